```python
import math
import jax
import jax.numpy as jnp
from jax import lax
import numpy as np

D_MODEL = 2048
BATCH = 4
SEQ = 2048
DEPTH = 2
DEC_BATCH = 128
DEC_SEQ = 1
PAST_LEN = 2048
PAGE_SIZE = 128

N_META = 16
NORM_EPS = 1e-6
POOL_WIDTH = D_MODEL // 2
POOL_WINDOWS = (2, 4, 8, 16)
POOL_GROUP = POOL_WIDTH // len(POOL_WINDOWS)
POOL_CTX = max(POOL_WINDOWS) - 1
ATTN_WIDTH = D_MODEL // 2
DA_HEADS = 8
DA_VDIM = ATTN_WIDTH // DA_HEADS
DA_QKDIM = DA_VDIM // 2
DA_SCALE = DA_QKDIM ** -0.5
ROT_DIM = DA_QKDIM // 4
ROPE_THETA = 500000.0
LAMBDA_INIT = 0.8 - 0.6 * math.exp(-0.3 * 0)
Q_BLOCK = 128
EVEN_SPLITS = [POOL_WIDTH, 2 * POOL_WIDTH, 2 * POOL_WIDTH + ATTN_WIDTH,
               2 * POOL_WIDTH + 2 * ATTN_WIDTH, 2 * POOL_WIDTH + 3 * ATTN_WIDTH]
EVEN_IN = 2 * POOL_WIDTH + 4 * ATTN_WIDTH
GLA_HEADS = 4
GLA_KW = D_MODEL // 2
GLA_VW = D_MODEL
GLA_KDIM = GLA_KW // GLA_HEADS
GLA_VDIM = GLA_VW // GLA_HEADS
GLA_RANK = 16
GLA_TAU = 16.0
GLA_CHUNK = 64
ODD_SPLITS = [GLA_KW, 2 * GLA_KW, 2 * GLA_KW + GLA_VW, 2 * GLA_KW + 2 * GLA_VW]
ODD_IN = 2 * GLA_KW + 2 * GLA_VW + GLA_RANK

kernel_name = 'meta_pool_diffattn_gla_decode_step'


def rms_norm(x, g):
    xf = x.astype(jnp.float32)
    y = xf * lax.rsqrt(jnp.mean(xf * xf, axis=-1, keepdims=True) + NORM_EPS)
    return (y * g.astype(jnp.float32)).astype(x.dtype)


def partial_rope(x, positions):
    half = ROT_DIM // 2
    inv = jnp.power(ROPE_THETA, -jnp.arange(half, dtype=jnp.float32) * 2.0 / ROT_DIM)
    ang = positions.astype(jnp.float32)[:, None] * inv[None, :]
    cos = jnp.cos(ang)[None, :, None, None, :]
    sin = jnp.sin(ang)[None, :, None, None, :]
    xf = x.astype(jnp.float32)
    x1 = xf[..., :half]
    x2 = xf[..., half:ROT_DIM]
    out = jnp.concatenate([x1 * cos - x2 * sin, x2 * cos + x1 * sin, xf[..., ROT_DIM:]], axis=-1)
    return out.astype(x.dtype)


def multiscale_pool(u, prev, positions):
    T = u.shape[1]
    ext = jnp.concatenate([prev.astype(u.dtype), u], axis=1)
    cs = jnp.cumsum(ext.astype(jnp.float32), axis=1)
    cs = jnp.pad(cs, ((0, 0), (1, 0), (0, 0)))
    means = []
    for gi, w in enumerate(POOL_WINDOWS):
        c0, c1 = gi * POOL_GROUP, (gi + 1) * POOL_GROUP
        hi = cs[:, POOL_CTX + 1:POOL_CTX + 1 + T, c0:c1]
        lo = cs[:, POOL_CTX + 1 - w:POOL_CTX + 1 - w + T, c0:c1]
        cnt = jnp.minimum(w, positions + 1).astype(jnp.float32)
        means.append((hi - lo) / cnt[None, :, None])
    mean = jnp.concatenate(means, axis=-1)
    return (mean - u.astype(jnp.float32)).astype(u.dtype), ext[:, -POOL_CTX:]


def diff_attn_prompt(q, k, v, lam):
    B, L = q.shape[0], q.shape[1]
    n_blk = -(-L // Q_BLOCK)
    pad = n_blk * Q_BLOCK - L
    qb = jnp.pad(q, ((0, 0), (0, pad), (0, 0), (0, 0), (0, 0)))
    qb = qb.reshape(B, n_blk, Q_BLOCK, DA_HEADS, 2, DA_QKDIM).transpose(1, 0, 2, 3, 4, 5)
    starts = jnp.arange(n_blk, dtype=jnp.int32) * Q_BLOCK
    key_pos = jnp.arange(L, dtype=jnp.int32)

    def block(args):
        q_blk, start = args
        q_pos = start + jnp.arange(Q_BLOCK, dtype=jnp.int32)
        s = jnp.einsum('bqhmd,bkhmd->bhmqk', q_blk, k).astype(jnp.float32) * DA_SCALE
        s = jnp.where(key_pos[None, :] <= q_pos[:, None], s, -jnp.inf)
        p = jax.nn.softmax(s, axis=-1)
        pd = p[:, :, 0] - lam * p[:, :, 1]
        return jnp.einsum('bhqk,bkhe->bqhe', pd.astype(v.dtype), v)

    o = lax.map(block, (qb, starts))
    return o.transpose(1, 0, 2, 3, 4).reshape(B, n_blk * Q_BLOCK, DA_HEADS, DA_VDIM)[:, :L]


def diff_attn_sample(q, k, v, k_past, v_past, lam):
    S = q.shape[1]
    P = k_past.shape[1]
    s_past = jnp.einsum('bqhmd,bkhmd->bhmqk', q, k_past).astype(jnp.float32)
    s_new = jnp.einsum('bqhmd,bkhmd->bhmqk', q, k).astype(jnp.float32)
    s_new = jnp.where(jnp.tril(jnp.ones((S, S), dtype=bool)), s_new, -jnp.inf)
    p = jax.nn.softmax(jnp.concatenate([s_past, s_new], axis=-1) * DA_SCALE, axis=-1)
    pd = (p[:, :, 0] - lam * p[:, :, 1]).astype(v.dtype)
    return (jnp.einsum('bhqk,bkhe->bqhe', pd[..., :P], v_past)
            + jnp.einsum('bhqk,bkhe->bqhe', pd[..., P:], v))


def even_layer(h, positions, pool_prev, kv_past, pre_g, post_g, w_in, pool_w, pool_scale,
               lq1, lk1, lq2, lk2, subln_g, w_out):
    B, T, _ = h.shape
    z = rms_norm(h, pre_g) @ w_in
    u, pool_gate, q, k, v, attn_gate = jnp.split(z, EVEN_SPLITS, axis=-1)
    d, pool_state = multiscale_pool(u, pool_prev, positions)
    pool_out = jnp.einsum('btgc,gcd->btgd', d.reshape(B, T, len(POOL_WINDOWS), POOL_GROUP), pool_w)
    pool_out = pool_out.reshape(B, T, POOL_WIDTH) * pool_scale
    q = partial_rope(q.reshape(B, T, DA_HEADS, 2, DA_QKDIM), positions)
    k = partial_rope(k.reshape(B, T, DA_HEADS, 2, DA_QKDIM), positions)
    v = v.reshape(B, T, DA_HEADS, DA_VDIM)
    lam = (jnp.exp(jnp.sum(lq1.astype(jnp.float32) * lk1.astype(jnp.float32)))
           - jnp.exp(jnp.sum(lq2.astype(jnp.float32) * lk2.astype(jnp.float32))) + LAMBDA_INIT)
    if kv_past is None:
        o = diff_attn_prompt(q, k, v, lam)
    else:
        o = diff_attn_sample(q, k, v, kv_past[0], kv_past[1], lam)
    o = rms_norm(o, subln_g) * (1.0 - LAMBDA_INIT)
    mix = jnp.concatenate([pool_out * jax.nn.silu(pool_gate),
                           o.reshape(B, T, ATTN_WIDTH) * jax.nn.silu(attn_gate)], axis=-1)
    h = h + rms_norm(mix @ w_out, post_g)
    return h, k.reshape(B, T, DA_HEADS, 2 * DA_QKDIM), v, pool_state


def gla_scan(q, k, v, g, s0, chunk):
    B, T = q.shape[0], q.shape[1]
    n = T // chunk

    def to_chunks(a):
        return a.reshape(B, n, chunk, GLA_HEADS, a.shape[-1]).transpose(1, 0, 3, 2, 4)

    causal = jnp.tril(jnp.ones((chunk, chunk), dtype=bool))[:, :, None]

    def step(state, xs):
        qc, kc, vc, gc = xs
        b = jnp.cumsum(gc, axis=2)
        o_inter = jnp.einsum('bhtd,bhde->bhte', qc * jnp.exp(b), state)
        decay = jnp.exp(jnp.where(causal, b[:, :, :, None, :] - b[:, :, None, :, :], -jnp.inf))
        scores = jnp.einsum('bhtsd,bhsd->bhts', qc[:, :, :, None, :] * decay, kc)
        o = o_inter + jnp.einsum('bhts,bhse->bhte', scores, vc)
        b_last = b[:, :, -1:, :]
        state = (jnp.exp(b_last[:, :, 0, :])[..., None] * state
                 + jnp.einsum('bhsd,bhse->bhde', kc * jnp.exp(b_last - b), vc))
        return state, o

    state, o = lax.scan(step, s0.astype(jnp.float32),
                        (to_chunks(q), to_chunks(k), to_chunks(v), to_chunks(g)))
    o = o.transpose(1, 0, 3, 2, 4).reshape(B, T, GLA_HEADS, GLA_VDIM)
    return o, state


def odd_layer(h, s0, has_meta, pre_g, post_g, w_in, wa2, ba, norm_g, w_out):
    B, T, _ = h.shape
    z = rms_norm(h, pre_g) @ w_in
    q, k, v, gate, lr = jnp.split(z, ODD_SPLITS, axis=-1)
    q = q.reshape(B, T, GLA_HEADS, GLA_KDIM) * (GLA_KDIM ** -0.5)
    k = k.reshape(B, T, GLA_HEADS, GLA_KDIM)
    v = v.reshape(B, T, GLA_HEADS, GLA_VDIM)
    g = jax.nn.log_sigmoid((lr @ wa2 + ba).astype(jnp.float32)) / GLA_TAU
    g = g.reshape(B, T, GLA_HEADS, GLA_KDIM)
    if has_meta:
        o_m, st = gla_scan(q[:, :N_META], k[:, :N_META], v[:, :N_META], g[:, :N_META], s0, N_META)
        o_r, st = gla_scan(q[:, N_META:], k[:, N_META:], v[:, N_META:], g[:, N_META:], st, GLA_CHUNK)
        o = jnp.concatenate([o_m, o_r], axis=1)
    else:
        o, st = gla_scan(q, k, v, g, s0, T)
    o = rms_norm(o.astype(h.dtype), norm_g).reshape(B, T, GLA_VW) * jax.nn.silu(gate)
    h = h + rms_norm(o @ w_out, post_g)
    return h, st


def setup_inputs(seed: int = 0) -> dict:
    key = jax.random.key(seed)
    ks = jax.random.split(key, 32)
    f32 = jnp.float32
    n_pages = PAST_LEN // PAGE_SIZE
    n_used = DEC_BATCH * n_pages
    n_phys = n_used + max(1, n_used // 4)
    nrm = lambda i, shape, s: jax.random.normal(ks[i], shape, f32) * s
    gain = lambda i, n: 1.0 + 0.02 * jax.random.normal(ks[i], (n,), f32)
    page_table = jax.random.permutation(ks[6], n_phys)[:n_used].reshape(DEC_BATCH, n_pages).astype(jnp.int32)
    return {
        'x_prompt': nrm(0, (BATCH, SEQ, D_MODEL), 1.0),
        'x_sample': nrm(1, (DEC_BATCH, DEC_SEQ, D_MODEL), 1.0),
        'cache_k': nrm(2, (n_phys, PAGE_SIZE, DA_HEADS, 2 * DA_QKDIM), 1.0),
        'cache_v': nrm(3, (n_phys, PAGE_SIZE, DA_HEADS, DA_VDIM), 1.0),
        'state_pool': nrm(4, (DEC_BATCH, POOL_CTX, POOL_WIDTH), 1.0),
        'state_gla': nrm(5, (DEC_BATCH, GLA_HEADS, GLA_KDIM, GLA_VDIM), 1.0),
        'page_table': page_table,
        'meta_tokens': nrm(7, (N_META, D_MODEL), 1.0),
        'pre_norm_0': gain(8, D_MODEL),
        'post_norm_0': gain(9, D_MODEL),
        'w_in_0': nrm(10, (D_MODEL, EVEN_IN), D_MODEL ** -0.5),
        'pool_w_0': nrm(11, (len(POOL_WINDOWS), POOL_GROUP, POOL_GROUP), POOL_GROUP ** -0.5),
        'pool_scale_0': gain(12, POOL_WIDTH),
        'lambda_q1_0': nrm(13, (DA_QKDIM,), 0.1),
        'lambda_k1_0': nrm(14, (DA_QKDIM,), 0.1),
        'lambda_q2_0': nrm(15, (DA_QKDIM,), 0.1),
        'lambda_k2_0': nrm(16, (DA_QKDIM,), 0.1),
        'subln_0': gain(17, DA_VDIM),
        'w_out_0': nrm(18, (POOL_WIDTH + ATTN_WIDTH, D_MODEL), (POOL_WIDTH + ATTN_WIDTH) ** -0.5),
        'pre_norm_1': gain(19, D_MODEL),
        'post_norm_1': gain(20, D_MODEL),
        'w_in_1': nrm(21, (D_MODEL, ODD_IN), D_MODEL ** -0.5),
        'gla_wa2_1': nrm(22, (GLA_RANK, GLA_KW), GLA_RANK ** -0.5),
        'gla_ba_1': nrm(23, (GLA_KW,), 0.01),
        'gla_norm_1': gain(24, GLA_VDIM),
        'w_out_1': nrm(25, (GLA_VW, D_MODEL), GLA_VW ** -0.5),
    }


def reference(x_prompt, x_sample, cache_k, cache_v, state_pool, state_gla, page_table, meta_tokens,
              pre_norm_0, post_norm_0, w_in_0, pool_w_0, pool_scale_0, lambda_q1_0, lambda_k1_0,
              lambda_q2_0, lambda_k2_0, subln_0, w_out_0,
              pre_norm_1, post_norm_1, w_in_1, gla_wa2_1, gla_ba_1, gla_norm_1, w_out_1):
    B = x_prompt.shape[0]
    Bd, S = x_sample.shape[0], x_sample.shape[1]
    meta = jnp.broadcast_to(meta_tokens[None].astype(x_prompt.dtype), (B, N_META, D_MODEL))
    h_p = jnp.concatenate([meta, x_prompt], axis=1)
    L = h_p.shape[1]
    pos_p = jnp.arange(L, dtype=jnp.int32)
    pos_s = PAST_LEN + jnp.arange(S, dtype=jnp.int32)
    n_pages = page_table.shape[1]
    k_past = cache_k[page_table].reshape(Bd, n_pages * PAGE_SIZE, DA_HEADS, 2, DA_QKDIM)
    v_past = cache_v[page_table].reshape(Bd, n_pages * PAGE_SIZE, DA_HEADS, DA_VDIM)
    h_s = x_sample
    for layer in range(DEPTH):
        if layer % 2 == 0:
            pool_zero = jnp.zeros((B, POOL_CTX, POOL_WIDTH), h_p.dtype)
            h_p, k_new_p, v_new_p, pool_new_p = even_layer(
                h_p, pos_p, pool_zero, None, pre_norm_0, post_norm_0, w_in_0, pool_w_0, pool_scale_0,
                lambda_q1_0, lambda_k1_0, lambda_q2_0, lambda_k2_0, subln_0, w_out_0)
            h_s, k_new_s, v_new_s, pool_new_s = even_layer(
                h_s, pos_s, state_pool, (k_past, v_past), pre_norm_0, post_norm_0, w_in_0, pool_w_0,
                pool_scale_0, lambda_q1_0, lambda_k1_0, lambda_q2_0, lambda_k2_0, subln_0, w_out_0)
        else:
            s_zero = jnp.zeros((B, GLA_HEADS, GLA_KDIM, GLA_VDIM), jnp.float32)
            h_p, gla_new_p = odd_layer(h_p, s_zero, True, pre_norm_1, post_norm_1, w_in_1,
                                       gla_wa2_1, gla_ba_1, gla_norm_1, w_out_1)
            h_s, gla_new_s = odd_layer(h_s, state_gla, False, pre_norm_1, post_norm_1, w_in_1,
                                       gla_wa2_1, gla_ba_1, gla_norm_1, w_out_1)
    gla_new_p = gla_new_p.astype(state_gla.dtype)
    gla_new_s = gla_new_s.astype(state_gla.dtype)
    return (h_p[:, N_META:], h_s, k_new_p, v_new_p, pool_new_p, gla_new_p,
            k_new_s, v_new_s, pool_new_s, gla_new_s)
```

```python
import functools
import math

import numpy as np
import jax
import jax.numpy as jnp
from jax import lax
from jax.experimental import pallas as pl
from jax.experimental.pallas import tpu as pltpu

f32 = jnp.float32
bf16 = jnp.bfloat16

D_MODEL = 2048
N_META = 16
NORM_EPS = 1e-6
POOL_WINDOWS = (2, 4, 8, 16)
POOL_GROUP = 256
POOL_CTX = 15
DA_HEADS = 8
DA_VDIM = 128
DA_QKDIM = 64
DA_SCALE = DA_QKDIM ** -0.5
ROT_DIM = 16
ROPE_THETA = 500000.0
LAMBDA_INIT = 0.8 - 0.6 * math.exp(-0.3 * 0)
GLA_HEADS = 4
GLA_KDIM = 256
GLA_VDIM = 512
GLA_RANK = 16
GLA_TAU = 16.0
GLA_CHUNK = 64
PAGE = 128

LANES = 128
ROW_TILE = 688
ATTN_TQ = 256
VMEM_LIMIT = 52 * 1024 * 1024

NT_DIMS = (((1,), (1,)), ((), ()))
TN_DIMS = (((0,), (0,)), ((), ()))


def _cparams(sem):
    return pltpu.CompilerParams(dimension_semantics=sem, vmem_limit_bytes=VMEM_LIMIT)


def _sigmoid(x):
    return 1.0 / (1.0 + jnp.exp(-x))


def _rms_rows(x, g):
    ms = jnp.mean(x * x, axis=-1, keepdims=True)
    return x * lax.rsqrt(ms + NORM_EPS) * g


def _rope_store(dst_ref, a, c, s1, s2, scale):
    for h in range(DA_HEADS):
        xh = a[:, h * LANES:(h + 1) * LANES]
        r = xh * c + pltpu.roll(xh, 8, 1) * s1 + pltpu.roll(xh, LANES - 8, 1) * s2
        dst_ref[:, h * LANES:(h + 1) * LANES] = r * scale if scale != 1.0 else r


def _inproj0_kernel(x_ref, g_ref, w_ref, c_ref, s1_ref, s2_ref, z_ref, k_ref, v_ref, xn_ref):
    j = pl.program_id(1)

    @pl.when(j == 0)
    def _():
        xn_ref[...] = _rms_rows(x_ref[...], g_ref[...]).astype(bf16)

    acc = jnp.dot(xn_ref[...], w_ref[...], preferred_element_type=f32)

    @pl.when((j == 0) | (j == 1) | (j == 3))
    def _():
        z_ref[...] = acc

    @pl.when(j == 2)
    def _():
        _rope_store(z_ref, acc, c_ref[...], s1_ref[...], s2_ref[...], DA_SCALE)

    @pl.when(j == 4)
    def _():
        _rope_store(k_ref, acc, c_ref[...], s1_ref[...], s2_ref[...], 1.0)

    @pl.when(j == 5)
    def _():
        v_ref[...] = acc


def _inproj0(x, g, w, tabs, tm, tab_blocks):
    m = x.shape[0]
    ni = m // tm
    c, s1, s2 = tabs
    tab_spec = pl.BlockSpec((tm, LANES), lambda i, j: (i % tab_blocks, 0))
    return pl.pallas_call(
        _inproj0_kernel,
        grid=(ni, 6),
        in_specs=[
            pl.BlockSpec((tm, D_MODEL), lambda i, j: (i, 0)),
            pl.BlockSpec((1, D_MODEL), lambda i, j: (0, 0)),
            pl.BlockSpec((D_MODEL, 1024), lambda i, j: (0, j)),
            tab_spec, tab_spec, tab_spec,
        ],
        out_specs=[
            pl.BlockSpec((tm, 1024), lambda i, j: (i, jnp.minimum(j, 3))),
            pl.BlockSpec((tm, 1024), lambda i, j: (i, 0)),
            pl.BlockSpec((tm, 1024), lambda i, j: (i, 0)),
        ],
        out_shape=[
            jax.ShapeDtypeStruct((m, 4096), f32),
            jax.ShapeDtypeStruct((m, 1024), f32),
            jax.ShapeDtypeStruct((m, 1024), f32),
        ],
        scratch_shapes=[pltpu.VMEM((tm, D_MODEL), bf16)],
        compiler_params=_cparams(("arbitrary", "arbitrary")),
        name="inproj0",
    )(x, g, w, c, s1, s2)


def _inproj1_kernel(x_ref, g_ref, w_ref, wlr_ref, z_ref, lr_ref, xn_ref):
    j = pl.program_id(1)

    @pl.when(j == 0)
    def _():
        xn_ref[...] = _rms_rows(x_ref[...], g_ref[...]).astype(bf16)
        lr_ref[...] = jnp.dot(xn_ref[...], wlr_ref[...], preferred_element_type=f32)

    acc = jnp.dot(xn_ref[...], w_ref[...], preferred_element_type=f32)

    @pl.when(j == 0)
    def _():
        z_ref[...] = acc * (GLA_KDIM ** -0.5)

    @pl.when(j != 0)
    def _():
        z_ref[...] = acc


def _inproj1(x, g, w, wlr, tm):
    m = x.shape[0]
    return pl.pallas_call(
        _inproj1_kernel,
        grid=(m // tm, 6),
        in_specs=[
            pl.BlockSpec((tm, D_MODEL), lambda i, j: (i, 0)),
            pl.BlockSpec((1, D_MODEL), lambda i, j: (0, 0)),
            pl.BlockSpec((D_MODEL, 1024), lambda i, j: (0, j)),
            pl.BlockSpec((D_MODEL, LANES), lambda i, j: (0, 0)),
        ],
        out_specs=[
            pl.BlockSpec((tm, 1024), lambda i, j: (i, j)),
            pl.BlockSpec((tm, LANES), lambda i, j: (i, 0)),
        ],
        out_shape=[
            jax.ShapeDtypeStruct((m, 6144), f32),
            jax.ShapeDtypeStruct((m, LANES), f32),
        ],
        scratch_shapes=[pltpu.VMEM((tm, D_MODEL), bf16)],
        compiler_params=_cparams(("arbitrary", "arbitrary")),
        name="inproj1",
    )(x, g, w, wlr)


def _outproj_kernel(*refs, n_parts):
    mix_refs = refs[:n_parts]
    w_ref, h_ref, g_ref, o_ref = refs[n_parts:]
    kw = w_ref.shape[0] // n_parts
    y = None
    for p in range(n_parts):
        t = jnp.dot(mix_refs[p][...], w_ref[p * kw:(p + 1) * kw, :], preferred_element_type=f32)
        y = t if y is None else y + t
    o_ref[...] = h_ref[...] + _rms_rows(y, g_ref[...])


def _outproj(mix_parts, w, h, g, tm):
    m = h.shape[0]
    n_parts = len(mix_parts)
    kw = w.shape[0] // n_parts
    return pl.pallas_call(
        functools.partial(_outproj_kernel, n_parts=n_parts),
        grid=(m // tm,),
        in_specs=[pl.BlockSpec((tm, kw), lambda i: (i, 0)) for _ in range(n_parts)] + [
            pl.BlockSpec(w.shape, lambda i: (0, 0), pipeline_mode=pl.Buffered(1)),
            pl.BlockSpec((tm, D_MODEL), lambda i: (i, 0)),
            pl.BlockSpec((1, D_MODEL), lambda i: (0, 0)),
        ],
        out_specs=pl.BlockSpec((tm, D_MODEL), lambda i: (i, 0)),
        out_shape=jax.ShapeDtypeStruct((m, D_MODEL), f32),
        compiler_params=_cparams(("arbitrary",)),
        name="outproj",
    )(*mix_parts, w, h, g)


def _pool_finish(window_sum, u, cnt, pw, scale, gate):
    d = window_sum / cnt - u
    po = jnp.dot(d.astype(bf16), pw, preferred_element_type=f32) * scale
    return (po * (gate * _sigmoid(gate))).astype(bf16)


def _pool_prompt_kernel(u_ref, halo_ref, pg_ref, pw_ref, sc_ref, o_ref, ext_ref, *, tiles_per_seq):
    tm = u_ref.shape[0]
    r = pl.program_id(0) % tiles_per_seq
    halo = halo_ref[...]
    ext_ref[0:16, :] = jnp.where(r == 0, jnp.zeros_like(halo), halo)
    ext_ref[16:, :] = u_ref[...]
    pos = r * tm + lax.broadcasted_iota(jnp.int32, (tm, 1), 0)
    for gi, w in enumerate(POOL_WINDOWS):
        cs = slice(gi * POOL_GROUP, (gi + 1) * POOL_GROUP)
        u = ext_ref[16:16 + tm, cs]
        s = u
        for t in range(1, w):
            s = s + ext_ref[16 - t:16 - t + tm, cs]
        cnt = jnp.minimum(w, pos + 1).astype(f32)
        o_ref[:, cs] = _pool_finish(s, u, cnt, pw_ref[gi], sc_ref[:, cs], pg_ref[:, cs])


def _pool_prompt(z4, pool_w, pool_scale, tm, tiles_per_seq):
    m = z4.shape[0]
    hb = tm // 16
    return pl.pallas_call(
        functools.partial(_pool_prompt_kernel, tiles_per_seq=tiles_per_seq),
        grid=(m // tm,),
        in_specs=[
            pl.BlockSpec((tm, 1024), lambda i: (i, 0)),
            pl.BlockSpec((16, 1024), lambda i: (jnp.maximum(i * hb - 1, 0), 0)),
            pl.BlockSpec((tm, 1024), lambda i: (i, 1)),
            pl.BlockSpec((4, POOL_GROUP, POOL_GROUP), lambda i: (0, 0, 0)),
            pl.BlockSpec((1, 1024), lambda i: (0, 0)),
        ],
        out_specs=pl.BlockSpec((tm, 1024), lambda i: (i, 0)),
        out_shape=jax.ShapeDtypeStruct((m, 1024), bf16),
        scratch_shapes=[pltpu.VMEM((tm + 16, 1024), f32)],
        compiler_params=_cparams(("arbitrary",)),
        name="pool_prompt",
    )(z4, z4, z4, pool_w, pool_scale)


def _pool_sample_kernel(u_ref, prev_ref, pg_ref, pw_ref, sc_ref, o_ref):
    for gi, w in enumerate(POOL_WINDOWS):
        cs = slice(gi * POOL_GROUP, (gi + 1) * POOL_GROUP)
        u = u_ref[:, cs]
        s = u
        for t in range(1, w):
            base = (POOL_CTX - t) * 1024 + gi * POOL_GROUP
            s = s + prev_ref[:, base:base + POOL_GROUP]
        o_ref[:, cs] = _pool_finish(s, u, float(w), pw_ref[gi], sc_ref[:, cs], pg_ref[:, cs])


def _pool_sample(z4, prev_flat, pool_w, pool_scale):
    m = z4.shape[0]
    return pl.pallas_call(
        _pool_sample_kernel,
        grid=(1,),
        in_specs=[
            pl.BlockSpec((m, 1024), lambda i: (0, 0)),
            pl.BlockSpec(prev_flat.shape, lambda i: (0, 0)),
            pl.BlockSpec((m, 1024), lambda i: (0, 1)),
            pl.BlockSpec((4, POOL_GROUP, POOL_GROUP), lambda i: (0, 0, 0)),
            pl.BlockSpec((1, 1024), lambda i: (0, 0)),
        ],
        out_specs=pl.BlockSpec((m, 1024), lambda i: (0, 0)),
        out_shape=jax.ShapeDtypeStruct((m, 1024), bf16),
        compiler_params=_cparams(("arbitrary",)),
        name="pool_sample",
    )(z4, prev_flat, z4, pool_w, pool_scale)


def _lambda(lq1, lk1, lq2, lk2):
    a = jnp.sum(lq1[...] * lk1[...], axis=-1, keepdims=True)
    b = jnp.sum(lq2[...] * lk2[...], axis=-1, keepdims=True)
    return jnp.exp(a) - jnp.exp(b) + LAMBDA_INIT


def _attn_finish(o, sub, gate):
    on = _rms_rows(o, sub) * (1.0 - LAMBDA_INIT)
    return (on * (gate * _sigmoid(gate))).astype(bf16)


def _attn_prompt_kernel(q_ref, k_ref, v_ref, ag_ref, lq1, lk1, lq2, lk2, sub_ref, o_ref,
                        kb_ref, vb_ref):
    kb_ref[...] = k_ref[...].astype(bf16)
    vb_ref[...] = v_ref[...].astype(bf16)
    lam = _lambda(lq1, lk1, lq2, lk2)
    first_map = lax.broadcasted_iota(jnp.int32, (1, LANES), 1) < DA_QKDIM
    n_tiles = (q_ref.shape[0] - N_META) // ATTN_TQ

    def stack_maps(q):
        return jnp.concatenate([jnp.where(first_map, q, 0.0), jnp.where(first_map, 0.0, q)],
                               axis=0).astype(bf16)

    def causal(t):
        row = lax.broadcasted_iota(jnp.int32, (2 * t, t), 0)
        col = lax.broadcasted_iota(jnp.int32, (2 * t, t), 1)
        return jnp.where(row >= t, row - t, row) >= col

    def step(qq, carry, kt, vt, mask):
        m, l, acc = carry
        s = lax.dot_general(qq, kt, NT_DIMS, preferred_element_type=f32)
        if mask is not None:
            s = jnp.where(mask, s, -jnp.inf)
        m_new = jnp.maximum(m, jnp.max(s, axis=-1, keepdims=True))
        alpha = jnp.exp(m - m_new)
        p = jnp.exp(s - m_new)
        l = alpha * l + jnp.sum(p, axis=-1, keepdims=True)
        acc = alpha * acc + jnp.dot(p.astype(bf16), vt, preferred_element_type=f32)
        return m_new, l, acc

    def init(t):
        return (jnp.full((2 * t, 1), -jnp.inf, f32), jnp.zeros((2 * t, 1), f32),
                jnp.zeros((2 * t, LANES), f32))

    def finish(carry, t, rows):
        _, l, acc = carry
        o = acc[:t] / l[:t] - lam * (acc[t:] / l[t:])
        o_ref[rows, :] = _attn_finish(o, sub_ref[...], ag_ref[rows, :])

    meta = slice(0, N_META)
    qq = stack_maps(q_ref[meta, :])
    carry = step(qq, init(N_META), kb_ref[meta, :], vb_ref[meta, :], causal(N_META))
    finish(carry, N_META, meta)

    diag_mask = causal(ATTN_TQ)

    def q_tile(i, _):
        r0 = pl.multiple_of(N_META + i * ATTN_TQ, 16)
        rows = pl.ds(r0, ATTN_TQ)
        qq = stack_maps(q_ref[rows, :])
        carry = step(qq, init(ATTN_TQ), kb_ref[meta, :], vb_ref[meta, :], None)

        def kv_tile(j, c):
            cols = pl.ds(pl.multiple_of(N_META + j * ATTN_TQ, 16), ATTN_TQ)
            return step(qq, c, kb_ref[cols, :], vb_ref[cols, :], None)

        carry = lax.fori_loop(0, i, kv_tile, carry)
        carry = step(qq, carry, kb_ref[rows, :], vb_ref[rows, :], diag_mask)
        finish(carry, ATTN_TQ, rows)
        return 0

    lax.fori_loop(0, n_tiles, q_tile, 0)


def _attn_prompt(z4, k, v, lams, subln, n_seq, seq_len):
    m = z4.shape[0]
    vec = pl.BlockSpec((1, DA_QKDIM), lambda b, h: (0, 0))
    return pl.pallas_call(
        _attn_prompt_kernel,
        grid=(n_seq, DA_HEADS),
        in_specs=[
            pl.BlockSpec((seq_len, LANES), lambda b, h: (b, 16 + h)),
            pl.BlockSpec((seq_len, LANES), lambda b, h: (b, h)),
            pl.BlockSpec((seq_len, LANES), lambda b, h: (b, h)),
            pl.BlockSpec((seq_len, LANES), lambda b, h: (b, 24 + h)),
            vec, vec, vec, vec,
            pl.BlockSpec((1, LANES), lambda b, h: (0, 0)),
        ],
        out_specs=pl.BlockSpec((seq_len, LANES), lambda b, h: (b, h)),
        out_shape=jax.ShapeDtypeStruct((m, 1024), bf16),
        scratch_shapes=[pltpu.VMEM((seq_len, LANES), bf16), pltpu.VMEM((seq_len, LANES), bf16)],
        compiler_params=_cparams(("arbitrary", "arbitrary")),
        name="attn_prompt",
    )(z4, k, v, z4, *lams, subln)


def _attn_sample_kernel(pt_ref, q_ref, kn_ref, vn_ref, ag_ref, lq1, lk1, lq2, lk2, sub_ref,
                        kp_ref, vp_ref, o_ref, m_ref, l_ref, acc_ref):
    del pt_ref
    p = pl.program_id(1)
    q = q_ref[0]
    first_map = lax.broadcasted_iota(jnp.int32, (DA_HEADS, LANES), 1) < DA_QKDIM

    @pl.when(p == 0)
    def _():
        prod = kn_ref[0] * q
        s0 = jnp.sum(jnp.where(first_map, prod, 0.0), axis=-1, keepdims=True)
        s1 = jnp.sum(jnp.where(first_map, 0.0, prod), axis=-1, keepdims=True)
        m_ref[0] = jnp.broadcast_to(s0, (DA_HEADS, LANES))
        m_ref[1] = jnp.broadcast_to(s1, (DA_HEADS, LANES))
        l_ref[...] = jnp.ones_like(l_ref)
        acc_ref[0] = vn_ref[0]
        acc_ref[1] = vn_ref[0]

    kp = kp_ref[0]
    vp = vp_ref[0]
    prod = kp * q[None]
    s_maps = (jnp.sum(jnp.where(first_map[None], prod, 0.0), axis=-1, keepdims=True),
              jnp.sum(jnp.where(first_map[None], 0.0, prod), axis=-1, keepdims=True))
    for mi in range(2):
        s = s_maps[mi]
        m_old = m_ref[mi]
        m_new = jnp.maximum(m_old, jnp.max(s, axis=0))
        alpha = jnp.exp(m_old - m_new)
        pw = jnp.exp(s - m_new[None, :, 0:1])
        l_ref[mi] = alpha * l_ref[mi] + jnp.sum(pw, axis=0)
        acc_ref[mi] = alpha * acc_ref[mi] + jnp.sum(pw * vp, axis=0)
        m_ref[mi] = m_new

    @pl.when(p == pl.num_programs(1) - 1)
    def _():
        lam = _lambda(lq1, lk1, lq2, lk2)
        o = acc_ref[0] / l_ref[0] - lam * (acc_ref[1] / l_ref[1])
        o_ref[0] = _attn_finish(o, sub_ref[...], ag_ref[0])


def _attn_sample(page_table, q, k_new, v_new, ag, lams, subln, cache_k, cache_v):
    bd, n_pages = page_table.shape
    tok = pl.BlockSpec((1, DA_HEADS, LANES), lambda b, p, pt: (b, 0, 0))
    vec = pl.BlockSpec((1, DA_QKDIM), lambda b, p, pt: (0, 0))
    page = pl.BlockSpec((1, PAGE, DA_HEADS, LANES), lambda b, p, pt: (pt[b * n_pages + p], 0, 0, 0))
    grid_spec = pltpu.PrefetchScalarGridSpec(
        num_scalar_prefetch=1,
        grid=(bd, n_pages),
        in_specs=[tok, tok, tok, tok, vec, vec, vec, vec,
                  pl.BlockSpec((1, LANES), lambda b, p, pt: (0, 0)), page, page],
        out_specs=tok,
        scratch_shapes=[pltpu.VMEM((2, DA_HEADS, LANES), f32) for _ in range(3)],
    )
    return pl.pallas_call(
        _attn_sample_kernel,
        grid_spec=grid_spec,
        out_shape=jax.ShapeDtypeStruct((bd, DA_HEADS, LANES), bf16),
        compiler_params=_cparams(("arbitrary", "arbitrary")),
        name="attn_sample",
    )(page_table.reshape(-1), q, k_new, v_new, ag, *lams, subln, cache_k, cache_v)


def _gla_consts(c):
    n = int(round(math.log2(c)))
    t = np.arange(c)[:, None]
    u = np.arange(c)[None, :]
    sums = [(u <= t), (u > t)]
    masks = [(u == t)]
    for lvl in range(n):
        half = c >> (lvl + 1)
        par = 2 * half
        split = (t // par) * par + half - 1
        upper = (t % par) >= half
        sums.append(upper & (u > split) & (u <= t))
        sums.append((~upper) & (u > t) & (u <= split))
        masks.append(upper & ((u % par) < half) & ((u // par) == (t // par)))
    return (np.stack(sums).astype(np.float32).reshape(-1, c),
            np.stack(masks).astype(np.float32))


def _log_decay(lr, wa, ba):
    x = jnp.dot(lr.astype(bf16), wa, preferred_element_type=f32) + ba
    return (jnp.minimum(x, 0.0) - jnp.log1p(jnp.exp(-jnp.abs(x)))) * (1.0 / GLA_TAU)


def _gla_finish(o, gn, gate):
    return (_rms_rows(o, gn) * (gate * _sigmoid(gate))).astype(bf16)


def _gla_prompt_kernel(q_ref, k_ref, v_ref, gate_ref, lr_ref, wa_ref, ba_ref, gn_ref,
                       sum16_ref, msk16_ref, sum64_ref, msk64_ref, o_ref, st_ref, g_ref, s_ref):
    g_ref[...] = _log_decay(lr_ref[...], wa_ref[...], ba_ref[...])
    s_ref[...] = jnp.zeros_like(s_ref)

    def chunk(r0, c, sum_ref, msk_ref):
        n = msk_ref.shape[0] - 1
        rows = pl.ds(r0, c)
        q = q_ref[rows, :]
        k = k_ref[rows, :]
        vb = v_ref[rows, :].astype(bf16)
        e = jnp.exp(jnp.dot(sum_ref[...], g_ref[rows, :], preferred_element_type=f32,
                            precision=lax.Precision.HIGHEST))
        st = s_ref[...]
        o = lax.dot_general((q * e[0:c]).astype(bf16), st.astype(bf16), NT_DIMS,
                            preferred_element_type=f32)
        sc = msk_ref[0] * lax.dot_general(q.astype(bf16), k.astype(bf16), NT_DIMS,
                                          preferred_element_type=f32)
        for lvl in range(n):
            ql = (q * e[(2 + 2 * lvl) * c:(3 + 2 * lvl) * c]).astype(bf16)
            kl = (k * e[(3 + 2 * lvl) * c:(4 + 2 * lvl) * c]).astype(bf16)
            sc = sc + msk_ref[1 + lvl] * lax.dot_general(ql, kl, NT_DIMS, preferred_element_type=f32)
        o = o + jnp.dot(sc.astype(bf16), vb, preferred_element_type=f32)
        k_end = (k * e[c:2 * c]).astype(bf16)
        s_ref[...] = st * e[c - 1:c] + lax.dot_general(vb, k_end, TN_DIMS, preferred_element_type=f32)
        o_ref[rows, :] = _gla_finish(o, gn_ref[...], gate_ref[rows, :])

    chunk(0, N_META, sum16_ref, msk16_ref)

    def body(ci, _):
        chunk(pl.multiple_of(N_META + ci * GLA_CHUNK, 16), GLA_CHUNK, sum64_ref, msk64_ref)
        return 0

    lax.fori_loop(0, (q_ref.shape[0] - N_META) // GLA_CHUNK, body, 0)
    st_ref[0, 0] = s_ref[...].T


def _gla_prompt(z1, lr, wa, ba, gn, n_seq, seq_len):
    m = z1.shape[0]
    sum16, msk16 = _gla_consts(N_META)
    sum64, msk64 = _gla_consts(GLA_CHUNK)
    consts = [jnp.asarray(a) for a in (sum16, msk16, sum64, msk64)]

    def full(a):
        nd = a.ndim
        return pl.BlockSpec(a.shape, lambda b, h: (0,) * nd)

    return pl.pallas_call(
        _gla_prompt_kernel,
        grid=(n_seq, GLA_HEADS),
        in_specs=[
            pl.BlockSpec((seq_len, GLA_KDIM), lambda b, h: (b, h)),
            pl.BlockSpec((seq_len, GLA_KDIM), lambda b, h: (b, 4 + h)),
            pl.BlockSpec((seq_len, GLA_VDIM), lambda b, h: (b, 4 + h)),
            pl.BlockSpec((seq_len, GLA_VDIM), lambda b, h: (b, 8 + h)),
            pl.BlockSpec((seq_len, LANES), lambda b, h: (b, 0)),
            pl.BlockSpec((LANES, GLA_KDIM), lambda b, h: (0, h)),
            pl.BlockSpec((1, GLA_KDIM), lambda b, h: (0, h)),
            pl.BlockSpec((1, GLA_VDIM), lambda b, h: (0, 0)),
        ] + [full(a) for a in consts],
        out_specs=[
            pl.BlockSpec((seq_len, GLA_VDIM), lambda b, h: (b, h)),
            pl.BlockSpec((1, 1, GLA_KDIM, GLA_VDIM), lambda b, h: (b, h, 0, 0)),
        ],
        out_shape=[
            jax.ShapeDtypeStruct((m, GLA_HEADS * GLA_VDIM), bf16),
            jax.ShapeDtypeStruct((n_seq, GLA_HEADS, GLA_KDIM, GLA_VDIM), f32),
        ],
        scratch_shapes=[pltpu.VMEM((seq_len, GLA_KDIM), f32), pltpu.VMEM((GLA_VDIM, GLA_KDIM), f32)],
        compiler_params=_cparams(("arbitrary", "arbitrary")),
        name="gla_prompt",
    )(z1, z1, z1, z1, lr, wa, ba, gn, *consts)


def _gla_sample_kernel(q_ref, k_ref, v_ref, gate_ref, lr_ref, wa_ref, ba_ref, gn_ref, s_ref,
                       o_ref, st_ref):
    q = q_ref[0]
    k = k_ref[0]
    v = v_ref[0]
    eg = jnp.exp(_log_decay(lr_ref[0], wa_ref[...], ba_ref[...]))
    eye = (lax.broadcasted_iota(jnp.int32, (GLA_KDIM, GLA_KDIM), 0)
           == lax.broadcasted_iota(jnp.int32, (GLA_KDIM, GLA_KDIM), 1))

    def as_column(row):
        return jnp.sum(jnp.where(eye, row, 0.0), axis=-1, keepdims=True)

    s = s_ref[0, 0]
    o = jnp.sum(as_column(q * eg) * s, axis=0, keepdims=True)
    o = o + jnp.sum(q * k, axis=-1, keepdims=True) * v
    st_ref[0, 0] = as_column(eg) * s + as_column(k) * v
    o_ref[0] = _gla_finish(o, gn_ref[...], gate_ref[0])


def _gla_sample(z1, lr, wa, ba, gn, state):
    bd = state.shape[0]
    return pl.pallas_call(
        _gla_sample_kernel,
        grid=(bd, GLA_HEADS),
        in_specs=[
            pl.BlockSpec((1, 1, GLA_KDIM), lambda b, h: (b, 0, h)),
            pl.BlockSpec((1, 1, GLA_KDIM), lambda b, h: (b, 0, 4 + h)),
            pl.BlockSpec((1, 1, GLA_VDIM), lambda b, h: (b, 0, 4 + h)),
            pl.BlockSpec((1, 1, GLA_VDIM), lambda b, h: (b, 0, 8 + h)),
            pl.BlockSpec((1, 1, LANES), lambda b, h: (b, 0, 0)),
            pl.BlockSpec((LANES, GLA_KDIM), lambda b, h: (0, h)),
            pl.BlockSpec((1, GLA_KDIM), lambda b, h: (0, h)),
            pl.BlockSpec((1, GLA_VDIM), lambda b, h: (0, 0)),
            pl.BlockSpec((1, 1, GLA_KDIM, GLA_VDIM), lambda b, h: (b, h, 0, 0)),
        ],
        out_specs=[
            pl.BlockSpec((1, 1, GLA_VDIM), lambda b, h: (b, 0, h)),
            pl.BlockSpec((1, 1, GLA_KDIM, GLA_VDIM), lambda b, h: (b, h, 0, 0)),
        ],
        out_shape=[
            jax.ShapeDtypeStruct((bd, 1, GLA_HEADS * GLA_VDIM), bf16),
            jax.ShapeDtypeStruct(state.shape, f32),
        ],
        compiler_params=_cparams(("arbitrary", "arbitrary")),
        name="gla_sample",
    )(z1, z1, z1, z1, lr, wa, ba, gn, state)


def _rope_tables(positions):
    half = ROT_DIM // 2
    inv = jnp.power(ROPE_THETA, -jnp.arange(half, dtype=f32) * 2.0 / ROT_DIM)
    ang = positions.astype(f32)[:, None] * inv[None, :]
    cos, sin = jnp.cos(ang), jnp.sin(ang)
    t = positions.shape[0]
    rest = DA_QKDIM - ROT_DIM
    z8 = jnp.zeros((t, half), f32)
    zr = jnp.zeros((t, rest), f32)
    c = jnp.concatenate([cos, cos, jnp.ones((t, rest), f32)], axis=1)
    s1 = jnp.concatenate([z8, sin, zr], axis=1)
    s2 = jnp.concatenate([-sin, z8, zr], axis=1)
    return tuple(jnp.concatenate([a, a], axis=1) for a in (c, s1, s2))


def kernel(x_prompt, x_sample, cache_k, cache_v, state_pool, state_gla, page_table, meta_tokens,
           pre_norm_0, post_norm_0, w_in_0, pool_w_0, pool_scale_0, lambda_q1_0, lambda_k1_0,
           lambda_q2_0, lambda_k2_0, subln_0, w_out_0,
           pre_norm_1, post_norm_1, w_in_1, gla_wa2_1, gla_ba_1, gla_norm_1, w_out_1):
    B, seq = x_prompt.shape[0], x_prompt.shape[1]
    Bd = x_sample.shape[0]
    L = seq + N_META
    assert L % ROW_TILE == 0 and (L - N_META) % ATTN_TQ == 0 and x_sample.shape[1] == 1

    row = lambda a: a.reshape(1, -1)
    w0 = jnp.concatenate([w_in_0[:, 0:3072], w_in_0[:, 5120:6144], w_in_0[:, 3072:5120]],
                         axis=1).astype(bf16)
    w1 = w_in_1[:, :6144].astype(bf16)
    w1_lr = jnp.pad(w_in_1[:, 6144:], ((0, 0), (0, LANES - GLA_RANK))).astype(bf16)
    wa = jnp.pad(gla_wa2_1, ((0, LANES - GLA_RANK), (0, 0))).astype(bf16)
    wo0 = w_out_0.astype(bf16)
    wo1 = w_out_1.astype(bf16)
    pw = pool_w_0.astype(bf16)
    lams = [row(a) for a in (lambda_q1_0, lambda_k1_0, lambda_q2_0, lambda_k2_0)]

    meta = jnp.broadcast_to(meta_tokens[None].astype(x_prompt.dtype), (B, N_META, D_MODEL))
    h_p = jnp.concatenate([meta, x_prompt], axis=1).reshape(B * L, D_MODEL)
    h_s = x_sample.reshape(Bd, D_MODEL)
    tabs_p = _rope_tables(jnp.arange(L, dtype=jnp.int32))
    past_len = page_table.shape[1] * PAGE
    tabs_s = _rope_tables(jnp.full((Bd,), past_len, dtype=jnp.int32))

    z4_p, k_p, v_p = _inproj0(h_p, row(pre_norm_0), w0, tabs_p, ROW_TILE, L // ROW_TILE)
    mix_a = _pool_prompt(z4_p, pw, row(pool_scale_0), ROW_TILE, L // ROW_TILE)
    mix_b = _attn_prompt(z4_p, k_p, v_p, lams, row(subln_0), B, L)
    h_p = _outproj([mix_a, mix_b], wo0, h_p, row(post_norm_0), ROW_TILE)

    z4_s, k_s, v_s = _inproj0(h_s, row(pre_norm_0), w0, tabs_s, Bd, 1)
    mix_a_s = _pool_sample(z4_s, state_pool.reshape(Bd, POOL_CTX * 1024), pw, row(pool_scale_0))
    hd = lambda a: a.reshape(Bd, DA_HEADS, LANES)
    mix_b_s = _attn_sample(page_table, hd(z4_s[:, 2048:3072]), hd(k_s), hd(v_s),
                           hd(z4_s[:, 3072:4096]), lams, row(subln_0), cache_k, cache_v)
    h_s = _outproj([mix_a_s, mix_b_s.reshape(Bd, 1024)], wo0, h_s, row(post_norm_0), Bd)

    z1_p, lr_p = _inproj1(h_p, row(pre_norm_1), w1, w1_lr, ROW_TILE)
    mix1_p, gla_p = _gla_prompt(z1_p, lr_p, wa, row(gla_ba_1), row(gla_norm_1), B, L)
    h_p = _outproj([mix1_p], wo1, h_p, row(post_norm_1), ROW_TILE)

    z1_s, lr_s = _inproj1(h_s, row(pre_norm_1), w1, w1_lr, Bd)
    mix1_s, gla_s = _gla_sample(z1_s.reshape(Bd, 1, 6144), lr_s.reshape(Bd, 1, LANES), wa,
                                row(gla_ba_1), row(gla_norm_1), state_gla)
    h_s = _outproj([mix1_s.reshape(Bd, 2048)], wo1, h_s, row(post_norm_1), Bd)

    u_p = z4_p[:, :1024].reshape(B, L, 1024)
    pool_p = u_p[:, L - POOL_CTX:]
    pool_s = jnp.concatenate([state_pool[:, 1:], z4_s[:, None, :1024]], axis=1)
    y_p = h_p.reshape(B, L, D_MODEL)[:, N_META:]
    return (y_p, h_s.reshape(Bd, 1, D_MODEL),
            k_p.reshape(B, L, DA_HEADS, LANES), v_p.reshape(B, L, DA_HEADS, LANES), pool_p, gla_p,
            k_s.reshape(Bd, 1, DA_HEADS, LANES), v_s.reshape(Bd, 1, DA_HEADS, LANES), pool_s, gla_s)
```

```python
import functools
import math

import numpy as np
import jax
import jax.numpy as jnp
from jax import lax
from jax.experimental import pallas as pl
from jax.experimental.pallas import tpu as pltpu

f32 = jnp.float32
bf16 = jnp.bfloat16

D_MODEL = 2048
N_META = 16
NORM_EPS = 1e-6
POOL_WINDOWS = (2, 4, 8, 16)
POOL_GROUP = 256
POOL_CTX = 15
DA_HEADS = 8
DA_VDIM = 128
DA_QKDIM = 64
DA_SCALE = DA_QKDIM ** -0.5
ROT_DIM = 16
ROPE_THETA = 500000.0
LAMBDA_INIT = 0.8 - 0.6 * math.exp(-0.3 * 0)
GLA_HEADS = 4
GLA_KDIM = 256
GLA_VDIM = 512
GLA_RANK = 16
GLA_TAU = 16.0
GLA_CHUNK = 64
PAGE = 128

LANES = 128
ROW_TILE = 688
ATTN_TQ = 512
ATTN_SUB = 8
LOG2E = 1.4426950408889634
VMEM_LIMIT = 52 * 1024 * 1024

NT_DIMS = (((1,), (1,)), ((), ()))
TN_DIMS = (((0,), (0,)), ((), ()))


def _cparams(sem):
    return pltpu.CompilerParams(dimension_semantics=sem, vmem_limit_bytes=VMEM_LIMIT)


def _sigmoid(x):
    return 1.0 / (1.0 + jnp.exp(-x))


def _rms_rows(x, g):
    ms = jnp.mean(x * x, axis=-1, keepdims=True)
    return x * lax.rsqrt(ms + NORM_EPS) * g


def _rope_store(dst_ref, a, c, s1, s2, scale):
    for h in range(DA_HEADS):
        xh = a[:, h * LANES:(h + 1) * LANES]
        r = xh * c + pltpu.roll(xh, 8, 1) * s1 + pltpu.roll(xh, LANES - 8, 1) * s2
        dst_ref[:, h * LANES:(h + 1) * LANES] = r * scale if scale != 1.0 else r


def _inproj0_kernel(x_ref, g_ref, w_ref, c_ref, s1_ref, s2_ref, z_ref, k_ref, v_ref, xn_ref):
    j = pl.program_id(1)

    @pl.when(j == 0)
    def _():
        xn_ref[...] = _rms_rows(x_ref[...], g_ref[...]).astype(bf16)

    acc = jnp.dot(xn_ref[...], w_ref[...], preferred_element_type=f32)

    @pl.when((j == 0) | (j == 1) | (j == 3))
    def _():
        z_ref[...] = acc

    @pl.when(j == 2)
    def _():
        _rope_store(z_ref, acc, c_ref[...], s1_ref[...], s2_ref[...], DA_SCALE)

    @pl.when(j == 4)
    def _():
        _rope_store(k_ref, acc, c_ref[...], s1_ref[...], s2_ref[...], 1.0)

    @pl.when(j == 5)
    def _():
        v_ref[...] = acc


def _inproj0(x, g, w, tabs, tm, tab_blocks):
    m = x.shape[0]
    ni = m // tm
    c, s1, s2 = tabs
    tab_spec = pl.BlockSpec((tm, LANES), lambda i, j: (i % tab_blocks, 0))
    return pl.pallas_call(
        _inproj0_kernel,
        grid=(ni, 6),
        in_specs=[
            pl.BlockSpec((tm, D_MODEL), lambda i, j: (i, 0)),
            pl.BlockSpec((1, D_MODEL), lambda i, j: (0, 0)),
            pl.BlockSpec((D_MODEL, 1024), lambda i, j: (0, j)),
            tab_spec, tab_spec, tab_spec,
        ],
        out_specs=[
            pl.BlockSpec((tm, 1024), lambda i, j: (i, jnp.minimum(j, 3))),
            pl.BlockSpec((tm, 1024), lambda i, j: (i, 0)),
            pl.BlockSpec((tm, 1024), lambda i, j: (i, 0)),
        ],
        out_shape=[
            jax.ShapeDtypeStruct((m, 4096), f32),
            jax.ShapeDtypeStruct((m, 1024), f32),
            jax.ShapeDtypeStruct((m, 1024), f32),
        ],
        scratch_shapes=[pltpu.VMEM((tm, D_MODEL), bf16)],
        compiler_params=_cparams(("arbitrary", "arbitrary")),
        name="inproj0",
    )(x, g, w, c, s1, s2)


def _inproj1_kernel(x_ref, g_ref, w_ref, wlr_ref, z_ref, lr_ref, xn_ref):
    j = pl.program_id(1)

    @pl.when(j == 0)
    def _():
        xn_ref[...] = _rms_rows(x_ref[...], g_ref[...]).astype(bf16)
        lr_ref[...] = jnp.dot(xn_ref[...], wlr_ref[...], preferred_element_type=f32)

    acc = jnp.dot(xn_ref[...], w_ref[...], preferred_element_type=f32)

    @pl.when(j == 0)
    def _():
        z_ref[...] = acc * (GLA_KDIM ** -0.5)

    @pl.when(j != 0)
    def _():
        z_ref[...] = acc


def _inproj1(x, g, w, wlr, tm):
    m = x.shape[0]
    return pl.pallas_call(
        _inproj1_kernel,
        grid=(m // tm, 6),
        in_specs=[
            pl.BlockSpec((tm, D_MODEL), lambda i, j: (i, 0)),
            pl.BlockSpec((1, D_MODEL), lambda i, j: (0, 0)),
            pl.BlockSpec((D_MODEL, 1024), lambda i, j: (0, j)),
            pl.BlockSpec((D_MODEL, LANES), lambda i, j: (0, 0)),
        ],
        out_specs=[
            pl.BlockSpec((tm, 1024), lambda i, j: (i, j)),
            pl.BlockSpec((tm, LANES), lambda i, j: (i, 0)),
        ],
        out_shape=[
            jax.ShapeDtypeStruct((m, 6144), f32),
            jax.ShapeDtypeStruct((m, LANES), f32),
        ],
        scratch_shapes=[pltpu.VMEM((tm, D_MODEL), bf16)],
        compiler_params=_cparams(("arbitrary", "arbitrary")),
        name="inproj1",
    )(x, g, w, wlr)


def _outproj_kernel(*refs, n_parts):
    mix_refs = refs[:n_parts]
    w_ref, h_ref, g_ref, o_ref = refs[n_parts:]
    kw = w_ref.shape[0] // n_parts
    y = None
    for p in range(n_parts):
        t = jnp.dot(mix_refs[p][...], w_ref[p * kw:(p + 1) * kw, :], preferred_element_type=f32)
        y = t if y is None else y + t
    o_ref[...] = h_ref[...] + _rms_rows(y, g_ref[...])


def _outproj(mix_parts, w, h, g, tm):
    m = h.shape[0]
    n_parts = len(mix_parts)
    kw = w.shape[0] // n_parts
    return pl.pallas_call(
        functools.partial(_outproj_kernel, n_parts=n_parts),
        grid=(m // tm,),
        in_specs=[pl.BlockSpec((tm, kw), lambda i: (i, 0)) for _ in range(n_parts)] + [
            pl.BlockSpec(w.shape, lambda i: (0, 0), pipeline_mode=pl.Buffered(1)),
            pl.BlockSpec((tm, D_MODEL), lambda i: (i, 0)),
            pl.BlockSpec((1, D_MODEL), lambda i: (0, 0)),
        ],
        out_specs=pl.BlockSpec((tm, D_MODEL), lambda i: (i, 0)),
        out_shape=jax.ShapeDtypeStruct((m, D_MODEL), f32),
        compiler_params=_cparams(("arbitrary",)),
        name="outproj",
    )(*mix_parts, w, h, g)


def _pool_finish(window_sum, u, cnt, pw, scale, gate):
    d = window_sum / cnt - u
    po = jnp.dot(d.astype(bf16), pw, preferred_element_type=f32) * scale
    return (po * (gate * _sigmoid(gate))).astype(bf16)


def _pool_prompt_kernel(u_ref, halo_ref, pg_ref, pw_ref, sc_ref, o_ref, ext_ref, *, tiles_per_seq):
    tm = u_ref.shape[0]
    r = pl.program_id(0) % tiles_per_seq
    halo = halo_ref[...]
    ext_ref[0:16, :] = jnp.where(r == 0, jnp.zeros_like(halo), halo)
    ext_ref[16:, :] = u_ref[...]
    pos = r * tm + lax.broadcasted_iota(jnp.int32, (tm, 1), 0)
    for gi, w in enumerate(POOL_WINDOWS):
        cs = slice(gi * POOL_GROUP, (gi + 1) * POOL_GROUP)
        u = ext_ref[16:16 + tm, cs]
        s = u
        for t in range(1, w):
            s = s + ext_ref[16 - t:16 - t + tm, cs]
        cnt = jnp.minimum(w, pos + 1).astype(f32)
        o_ref[:, cs] = _pool_finish(s, u, cnt, pw_ref[gi], sc_ref[:, cs], pg_ref[:, cs])


def _pool_prompt(z4, pool_w, pool_scale, tm, tiles_per_seq):
    m = z4.shape[0]
    hb = tm // 16
    return pl.pallas_call(
        functools.partial(_pool_prompt_kernel, tiles_per_seq=tiles_per_seq),
        grid=(m // tm,),
        in_specs=[
            pl.BlockSpec((tm, 1024), lambda i: (i, 0)),
            pl.BlockSpec((16, 1024), lambda i: (jnp.maximum(i * hb - 1, 0), 0)),
            pl.BlockSpec((tm, 1024), lambda i: (i, 1)),
            pl.BlockSpec((4, POOL_GROUP, POOL_GROUP), lambda i: (0, 0, 0)),
            pl.BlockSpec((1, 1024), lambda i: (0, 0)),
        ],
        out_specs=pl.BlockSpec((tm, 1024), lambda i: (i, 0)),
        out_shape=jax.ShapeDtypeStruct((m, 1024), bf16),
        scratch_shapes=[pltpu.VMEM((tm + 16, 1024), f32)],
        compiler_params=_cparams(("arbitrary",)),
        name="pool_prompt",
    )(z4, z4, z4, pool_w, pool_scale)


def _pool_sample_kernel(u_ref, prev_ref, pg_ref, pw_ref, sc_ref, o_ref):
    for gi, w in enumerate(POOL_WINDOWS):
        cs = slice(gi * POOL_GROUP, (gi + 1) * POOL_GROUP)
        u = u_ref[:, cs]
        s = u
        for t in range(1, w):
            base = (POOL_CTX - t) * 1024 + gi * POOL_GROUP
            s = s + prev_ref[:, base:base + POOL_GROUP]
        o_ref[:, cs] = _pool_finish(s, u, float(w), pw_ref[gi], sc_ref[:, cs], pg_ref[:, cs])


def _pool_sample(z4, prev_flat, pool_w, pool_scale):
    m = z4.shape[0]
    return pl.pallas_call(
        _pool_sample_kernel,
        grid=(1,),
        in_specs=[
            pl.BlockSpec((m, 1024), lambda i: (0, 0)),
            pl.BlockSpec(prev_flat.shape, lambda i: (0, 0)),
            pl.BlockSpec((m, 1024), lambda i: (0, 1)),
            pl.BlockSpec((4, POOL_GROUP, POOL_GROUP), lambda i: (0, 0, 0)),
            pl.BlockSpec((1, 1024), lambda i: (0, 0)),
        ],
        out_specs=pl.BlockSpec((m, 1024), lambda i: (0, 0)),
        out_shape=jax.ShapeDtypeStruct((m, 1024), bf16),
        compiler_params=_cparams(("arbitrary",)),
        name="pool_sample",
    )(z4, prev_flat, z4, pool_w, pool_scale)


def _lambda(lq1, lk1, lq2, lk2):
    a = jnp.sum(lq1[...] * lk1[...], axis=-1, keepdims=True)
    b = jnp.sum(lq2[...] * lk2[...], axis=-1, keepdims=True)
    return jnp.exp(a) - jnp.exp(b) + LAMBDA_INIT


def _attn_finish(o, sub, gate):
    on = _rms_rows(o, sub) * (1.0 - LAMBDA_INIT)
    return (on * (gate * _sigmoid(gate))).astype(bf16)


def _attn_prompt_kernel(q_ref, k_ref, v_ref, ag_ref, lq1, lk1, lq2, lk2, sub_ref, o_ref,
                        kb_ref, vb_ref):
    kb_ref[...] = k_ref[...].astype(bf16)
    vb_ref[...] = v_ref[...].astype(bf16)
    lam = _lambda(lq1, lk1, lq2, lk2)
    first_map = lax.broadcasted_iota(jnp.int32, (1, LANES), 1) < DA_QKDIM
    n_tiles = (q_ref.shape[0] - N_META) // ATTN_TQ

    def stack_maps(q):
        q = q * LOG2E
        return jnp.concatenate([jnp.where(first_map, q, 0.0), jnp.where(first_map, 0.0, q)],
                               axis=0).astype(bf16)

    def causal(t):
        row = lax.broadcasted_iota(jnp.int32, (2 * t, t), 0)
        col = lax.broadcasted_iota(jnp.int32, (2 * t, t), 1)
        return jnp.where(row >= t, row - t, row) >= col

    def step(qq, carry, kt, vt, mask):
        m, l, acc = carry
        s = lax.dot_general(qq, kt, NT_DIMS, preferred_element_type=f32)
        if mask is not None:
            s = jnp.where(mask, s, -jnp.inf)
        m_new = jnp.maximum(m, jnp.max(s, axis=-1, keepdims=True))
        alpha = jnp.exp2(m - m_new)
        p = jnp.exp2(s - m_new)
        l = alpha * l + jnp.sum(p, axis=-1, keepdims=True)
        acc = alpha * acc + jnp.dot(p.astype(bf16), vt, preferred_element_type=f32)
        return m_new, l, acc

    def init(t):
        return (jnp.full((2 * t, 1), -jnp.inf, f32), jnp.zeros((2 * t, 1), f32),
                jnp.zeros((2 * t, LANES), f32))

    def finish(carry, t, rows):
        _, l, acc = carry
        o = acc[:t] / l[:t] - lam * (acc[t:] / l[t:])
        o_ref[rows, :] = _attn_finish(o, sub_ref[...], ag_ref[rows, :])

    meta = slice(0, N_META)
    qq = stack_maps(q_ref[meta, :])
    carry = step(qq, init(N_META), kb_ref[meta, :], vb_ref[meta, :], causal(N_META))
    finish(carry, N_META, meta)

    diag_mask = causal(ATTN_TQ)

    def q_tile(i, _):
        r0 = pl.multiple_of(N_META + i * ATTN_TQ, 16)
        rows = pl.ds(r0, ATTN_TQ)
        qq = stack_maps(q_ref[rows, :])
        carry = step(qq, init(ATTN_TQ), kb_ref[meta, :], vb_ref[meta, :], None)

        def kv_tile(j, c):
            cols = pl.ds(pl.multiple_of(N_META + j * ATTN_TQ, 16), ATTN_TQ)
            return step(qq, c, kb_ref[cols, :], vb_ref[cols, :], None)

        carry = lax.fori_loop(0, i, kv_tile, carry)
        carry = step(qq, carry, kb_ref[rows, :], vb_ref[rows, :], diag_mask)
        finish(carry, ATTN_TQ, rows)
        return 0

    lax.fori_loop(0, n_tiles, q_tile, 0)


def _attn_prompt(z4, k, v, lams, subln, n_seq, seq_len):
    m = z4.shape[0]
    vec = pl.BlockSpec((1, DA_QKDIM), lambda b, h: (0, 0))
    return pl.pallas_call(
        _attn_prompt_kernel,
        grid=(n_seq, DA_HEADS),
        in_specs=[
            pl.BlockSpec((seq_len, LANES), lambda b, h: (b, 16 + h)),
            pl.BlockSpec((seq_len, LANES), lambda b, h: (b, h)),
            pl.BlockSpec((seq_len, LANES), lambda b, h: (b, h)),
            pl.BlockSpec((seq_len, LANES), lambda b, h: (b, 24 + h)),
            vec, vec, vec, vec,
            pl.BlockSpec((1, LANES), lambda b, h: (0, 0)),
        ],
        out_specs=pl.BlockSpec((seq_len, LANES), lambda b, h: (b, h)),
        out_shape=jax.ShapeDtypeStruct((m, 1024), bf16),
        scratch_shapes=[pltpu.VMEM((seq_len, LANES), bf16), pltpu.VMEM((seq_len, LANES), bf16)],
        compiler_params=_cparams(("arbitrary", "arbitrary")),
        name="attn_prompt",
    )(z4, k, v, z4, *lams, subln)


def _half_sum_matrix():
    d = np.arange(LANES)[:, None] < DA_QKDIM
    c = np.arange(2 * LANES)[None, :]
    same = d == ((c % LANES) < DA_QKDIM)
    return np.where(c < LANES, same, ~same).astype(np.float32)


def _attn_sample_kernel(pt_ref, q_ref, kn_ref, vn_ref, ag_ref, lq1, lk1, lq2, lk2, sub_ref, e_ref,
                        *rest, pages):
    del pt_ref
    kp_refs, vp_refs = rest[:pages], rest[pages:2 * pages]
    o_ref, m_ref, l_ref, acca_ref, accb_ref = rest[2 * pages:]
    p = pl.program_id(1)
    q = q_ref[0] * LOG2E
    first_half = lax.broadcasted_iota(jnp.int32, (DA_HEADS, LANES), 1) < DA_QKDIM
    swap = lambda a: pltpu.roll(a, DA_QKDIM, 1)

    @pl.when(p == 0)
    def _():
        prod = kn_ref[0] * q
        s0 = jnp.sum(jnp.where(first_half, prod, 0.0), axis=-1, keepdims=True)
        s1 = jnp.sum(jnp.where(first_half, 0.0, prod), axis=-1, keepdims=True)
        m_ref[...] = jnp.where(first_half, s0, s1)
        l_ref[...] = jnp.ones_like(l_ref)
        acca_ref[...] = vn_ref[0]
        accb_ref[...] = vn_ref[0]

    m, l, acc_a, acc_b = m_ref[...], l_ref[...], acca_ref[...], accb_ref[...]
    for pg in range(pages):
        for st in range(PAGE // ATTN_SUB):
            rows = slice(st * ATTN_SUB, (st + 1) * ATTN_SUB)
            kp = kp_refs[pg][0, rows]
            vp = vp_refs[pg][0, rows]
            prod = (kp * q[None]).reshape(ATTN_SUB * DA_HEADS, LANES).astype(bf16)
            r = jnp.dot(prod, e_ref[...], preferred_element_type=f32)
            ra = r[:, :LANES].reshape(ATTN_SUB, DA_HEADS, LANES)
            rb = r[:, LANES:].reshape(ATTN_SUB, DA_HEADS, LANES)
            m_new = jnp.maximum(m, jnp.max(ra, axis=0))
            alpha = jnp.exp2(m - m_new)
            pa = jnp.exp2(ra - m_new[None])
            pb = jnp.exp2(rb - swap(m_new)[None])
            l = alpha * l + jnp.sum(pa, axis=0)
            acc_a = alpha * acc_a + jnp.sum(pa * vp, axis=0)
            acc_b = swap(alpha) * acc_b + jnp.sum(pb * vp, axis=0)
            m = m_new
    m_ref[...], l_ref[...], acca_ref[...], accb_ref[...] = m, l, acc_a, acc_b

    @pl.when(p == pl.num_programs(1) - 1)
    def _():
        lam = _lambda(lq1, lk1, lq2, lk2)
        l_sw = swap(l)
        o0 = jnp.where(first_half, acc_a, acc_b) / jnp.where(first_half, l, l_sw)
        o1 = jnp.where(first_half, acc_b, acc_a) / jnp.where(first_half, l_sw, l)
        o_ref[0] = _attn_finish(o0 - lam * o1, sub_ref[...], ag_ref[0])


def _attn_sample(page_table, q, k_new, v_new, ag, lams, subln, cache_k, cache_v, pages=4):
    bd, n_pages = page_table.shape
    assert n_pages % pages == 0
    tok = pl.BlockSpec((1, DA_HEADS, LANES), lambda b, p, pt: (b, 0, 0))
    vec = pl.BlockSpec((1, DA_QKDIM), lambda b, p, pt: (0, 0))

    def page(i):
        return pl.BlockSpec((1, PAGE, DA_HEADS, LANES),
                            lambda b, p, pt: (pt[b * n_pages + p * pages + i], 0, 0, 0))

    e = jnp.asarray(_half_sum_matrix(), dtype=bf16)
    grid_spec = pltpu.PrefetchScalarGridSpec(
        num_scalar_prefetch=1,
        grid=(bd, n_pages // pages),
        in_specs=[tok, tok, tok, tok, vec, vec, vec, vec,
                  pl.BlockSpec((1, LANES), lambda b, p, pt: (0, 0)),
                  pl.BlockSpec((LANES, 2 * LANES), lambda b, p, pt: (0, 0))]
                 + [page(i) for i in range(pages)] * 2,
        out_specs=tok,
        scratch_shapes=[pltpu.VMEM((DA_HEADS, LANES), f32) for _ in range(4)],
    )
    return pl.pallas_call(
        functools.partial(_attn_sample_kernel, pages=pages),
        grid_spec=grid_spec,
        out_shape=jax.ShapeDtypeStruct((bd, DA_HEADS, LANES), bf16),
        compiler_params=_cparams(("arbitrary", "arbitrary")),
        name="attn_sample",
    )(page_table.reshape(-1), q, k_new, v_new, ag, *lams, subln, e,
      *([cache_k] * pages), *([cache_v] * pages))


def _gla_consts(c):
    n = int(round(math.log2(c)))
    t = np.arange(c)[:, None]
    u = np.arange(c)[None, :]
    sums = [(u <= t), (u > t)]
    masks = [(u == t)]
    for lvl in range(n):
        half = c >> (lvl + 1)
        par = 2 * half
        split = (t // par) * par + half - 1
        upper = (t % par) >= half
        sums.append((upper & (u > split) & (u <= t)) | ((~upper) & (u > t) & (u <= split)))
        masks.append(upper & ((u % par) < half) & ((u // par) == (t // par)))
    sums = np.stack(sums).astype(np.float32).reshape(-1, c)
    return np.concatenate([sums] * 3, axis=1), np.stack(masks).astype(np.float32)


def _split3(x):
    hi = x.astype(bf16)
    r = x - hi.astype(f32)
    mid = r.astype(bf16)
    lo = (r - mid.astype(f32)).astype(bf16)
    return jnp.concatenate([hi, mid, lo], axis=0)


def _log_sigmoid(x):
    return jnp.minimum(x, 0.0) - jnp.log1p(jnp.exp(-jnp.abs(x)))


def _log_decay(lr, wa, ba):
    x = jnp.dot(lr.astype(bf16), wa, preferred_element_type=f32) + ba
    return _log_sigmoid(x) * (1.0 / GLA_TAU)


def _gla_finish(o, gn, gate):
    return (_rms_rows(o, gn) * (gate * _sigmoid(gate))).astype(bf16)


def _gla_prompt_kernel(q_ref, k_ref, v_ref, gate_ref, lr_ref, wa_ref, ba_ref, gn_ref,
                       sum16_ref, msk16_ref, sum64_ref, msk64_ref, o_ref, st_ref, g_ref, s_ref):
    g_ref[...] = _log_decay(lr_ref[...], wa_ref[...], ba_ref[...])
    s_ref[...] = jnp.zeros_like(s_ref)

    def chunk(r0, c, sum_ref, msk_ref):
        n = msk_ref.shape[0] - 1
        rows = pl.ds(r0, c)
        q = q_ref[rows, :]
        k = k_ref[rows, :]
        vb = v_ref[rows, :].astype(bf16)
        e = jnp.exp(jnp.dot(sum_ref[...], _split3(g_ref[rows, :]),
                            preferred_element_type=f32))
        st = s_ref[...]
        o = lax.dot_general((q * e[0:c]).astype(bf16), st.astype(bf16), NT_DIMS,
                            preferred_element_type=f32)
        sc = msk_ref[0] * lax.dot_general(q.astype(bf16), k.astype(bf16), NT_DIMS,
                                          preferred_element_type=f32)
        for lvl in range(n):
            el = e[(2 + lvl) * c:(3 + lvl) * c]
            sc = sc + msk_ref[1 + lvl] * lax.dot_general(
                (q * el).astype(bf16), (k * el).astype(bf16), NT_DIMS, preferred_element_type=f32)
        o = o + jnp.dot(sc.astype(bf16), vb, preferred_element_type=f32)
        k_end = (k * e[c:2 * c]).astype(bf16)
        s_ref[...] = st * e[c - 1:c] + lax.dot_general(vb, k_end, TN_DIMS, preferred_element_type=f32)
        o_ref[rows, :] = _gla_finish(o, gn_ref[...], gate_ref[rows, :])

    chunk(0, N_META, sum16_ref, msk16_ref)

    def body(ci, _):
        chunk(pl.multiple_of(N_META + ci * GLA_CHUNK, 16), GLA_CHUNK, sum64_ref, msk64_ref)
        return 0

    lax.fori_loop(0, (q_ref.shape[0] - N_META) // GLA_CHUNK, body, 0)
    st_ref[0, 0] = s_ref[...].T


def _gla_prompt(z1, lr, wa, ba, gn, n_seq, seq_len):
    m = z1.shape[0]
    sum16, msk16 = _gla_consts(N_META)
    sum64, msk64 = _gla_consts(GLA_CHUNK)
    consts = [jnp.asarray(sum16, bf16), jnp.asarray(msk16), jnp.asarray(sum64, bf16),
              jnp.asarray(msk64)]

    def full(a):
        nd = a.ndim
        return pl.BlockSpec(a.shape, lambda b, h: (0,) * nd)

    return pl.pallas_call(
        _gla_prompt_kernel,
        grid=(n_seq, GLA_HEADS),
        in_specs=[
            pl.BlockSpec((seq_len, GLA_KDIM), lambda b, h: (b, h)),
            pl.BlockSpec((seq_len, GLA_KDIM), lambda b, h: (b, 4 + h)),
            pl.BlockSpec((seq_len, GLA_VDIM), lambda b, h: (b, 4 + h)),
            pl.BlockSpec((seq_len, GLA_VDIM), lambda b, h: (b, 8 + h)),
            pl.BlockSpec((seq_len, LANES), lambda b, h: (b, 0)),
            pl.BlockSpec((LANES, GLA_KDIM), lambda b, h: (0, h)),
            pl.BlockSpec((1, GLA_KDIM), lambda b, h: (0, h)),
            pl.BlockSpec((1, GLA_VDIM), lambda b, h: (0, 0)),
        ] + [full(a) for a in consts],
        out_specs=[
            pl.BlockSpec((seq_len, GLA_VDIM), lambda b, h: (b, h)),
            pl.BlockSpec((1, 1, GLA_KDIM, GLA_VDIM), lambda b, h: (b, h, 0, 0)),
        ],
        out_shape=[
            jax.ShapeDtypeStruct((m, GLA_HEADS * GLA_VDIM), bf16),
            jax.ShapeDtypeStruct((n_seq, GLA_HEADS, GLA_KDIM, GLA_VDIM), f32),
        ],
        scratch_shapes=[pltpu.VMEM((seq_len, GLA_KDIM), f32), pltpu.VMEM((GLA_VDIM, GLA_KDIM), f32)],
        compiler_params=_cparams(("arbitrary", "arbitrary")),
        name="gla_prompt",
    )(z1, z1, z1, z1, lr, wa, ba, gn, *consts)


def _gla_sample_kernel(z_ref, lr_ref, wa_ref, wat_ref, ba_ref, bat_ref, gn_ref, s_ref,
                       o_ref, st_ref):
    lr = lr_ref[0].astype(bf16)
    lr_rows = jnp.broadcast_to(lr, (LANES, LANES))
    first_row = lax.broadcasted_iota(jnp.int32, (16, 1), 0) == 0
    kw, vw = GLA_HEADS * GLA_KDIM, GLA_HEADS * GLA_VDIM
    for h in range(GLA_HEADS):
        kc = slice(h * GLA_KDIM, (h + 1) * GLA_KDIM)
        vc = slice(h * GLA_VDIM, (h + 1) * GLA_VDIM)
        q = z_ref[0, :, kc]
        k = z_ref[0, :, kw + h * GLA_KDIM:kw + (h + 1) * GLA_KDIM]
        v = z_ref[0, :, 2 * kw + h * GLA_VDIM:2 * kw + (h + 1) * GLA_VDIM]
        gate = z_ref[0, :, 2 * kw + vw + h * GLA_VDIM:2 * kw + vw + (h + 1) * GLA_VDIM]
        x_row = jnp.dot(lr, wa_ref[:, kc], preferred_element_type=f32) + ba_ref[:, kc]
        x_col = lax.dot_general(wat_ref[kc, :], lr_rows, NT_DIMS,
                                preferred_element_type=f32) + bat_ref[kc, :]
        eg_row = jnp.exp(_log_sigmoid(x_row) * (1.0 / GLA_TAU))
        eg_col = jnp.exp(_log_sigmoid(x_col) * (1.0 / GLA_TAU))
        s = s_ref[0, h]
        o = jnp.dot((q * eg_row).astype(bf16), s.astype(bf16), preferred_element_type=f32)
        o = o + jnp.sum(q * k, axis=-1, keepdims=True) * v
        k16 = jnp.where(first_row, k, 0.0).astype(bf16)
        v16 = jnp.broadcast_to(v, (16, GLA_VDIM)).astype(bf16)
        st_ref[0, h] = (s * jnp.concatenate([eg_col] * (GLA_VDIM // LANES), axis=1)
                        + lax.dot_general(k16, v16, TN_DIMS, preferred_element_type=f32))
        o_ref[0, :, vc] = _gla_finish(o, gn_ref[...], gate)


def _gla_sample(z1, lr, wa, ba, gn, state):
    bd = state.shape[0]
    kw = GLA_HEADS * GLA_KDIM
    wat = wa.T
    bat = jnp.broadcast_to(ba.reshape(kw, 1), (kw, LANES))
    const = lambda a: pl.BlockSpec(a.shape, lambda b: (0,) * a.ndim)
    st_spec = pl.BlockSpec((1, GLA_HEADS, GLA_KDIM, GLA_VDIM), lambda b: (b, 0, 0, 0))
    return pl.pallas_call(
        _gla_sample_kernel,
        grid=(bd,),
        in_specs=[
            pl.BlockSpec((1, 1, z1.shape[2]), lambda b: (b, 0, 0)),
            pl.BlockSpec((1, 1, LANES), lambda b: (b, 0, 0)),
            const(wa), const(wat), const(ba), const(bat), const(gn), st_spec,
        ],
        out_specs=[
            pl.BlockSpec((1, 1, GLA_HEADS * GLA_VDIM), lambda b: (b, 0, 0)),
            st_spec,
        ],
        out_shape=[
            jax.ShapeDtypeStruct((bd, 1, GLA_HEADS * GLA_VDIM), bf16),
            jax.ShapeDtypeStruct(state.shape, f32),
        ],
        compiler_params=_cparams(("arbitrary",)),
        name="gla_sample",
    )(z1, lr, wa, wat, ba, bat, gn, state)


def _rope_tables(positions):
    half = ROT_DIM // 2
    inv = jnp.power(ROPE_THETA, -jnp.arange(half, dtype=f32) * 2.0 / ROT_DIM)
    ang = positions.astype(f32)[:, None] * inv[None, :]
    cos, sin = jnp.cos(ang), jnp.sin(ang)
    t = positions.shape[0]
    rest = DA_QKDIM - ROT_DIM
    z8 = jnp.zeros((t, half), f32)
    zr = jnp.zeros((t, rest), f32)
    c = jnp.concatenate([cos, cos, jnp.ones((t, rest), f32)], axis=1)
    s1 = jnp.concatenate([z8, sin, zr], axis=1)
    s2 = jnp.concatenate([-sin, z8, zr], axis=1)
    return tuple(jnp.concatenate([a, a], axis=1) for a in (c, s1, s2))


def kernel(x_prompt, x_sample, cache_k, cache_v, state_pool, state_gla, page_table, meta_tokens,
           pre_norm_0, post_norm_0, w_in_0, pool_w_0, pool_scale_0, lambda_q1_0, lambda_k1_0,
           lambda_q2_0, lambda_k2_0, subln_0, w_out_0,
           pre_norm_1, post_norm_1, w_in_1, gla_wa2_1, gla_ba_1, gla_norm_1, w_out_1):
    B, seq = x_prompt.shape[0], x_prompt.shape[1]
    Bd = x_sample.shape[0]
    L = seq + N_META
    assert L % ROW_TILE == 0 and (L - N_META) % ATTN_TQ == 0 and x_sample.shape[1] == 1

    row = lambda a: a.reshape(1, -1)
    w0 = jnp.concatenate([w_in_0[:, 0:3072], w_in_0[:, 5120:6144], w_in_0[:, 3072:5120]],
                         axis=1).astype(bf16)
    w1 = w_in_1[:, :6144].astype(bf16)
    w1_lr = jnp.pad(w_in_1[:, 6144:], ((0, 0), (0, LANES - GLA_RANK))).astype(bf16)
    wa = jnp.pad(gla_wa2_1, ((0, LANES - GLA_RANK), (0, 0))).astype(bf16)
    wo0 = w_out_0.astype(bf16)
    wo1 = w_out_1.astype(bf16)
    pw = pool_w_0.astype(bf16)
    lams = [row(a) for a in (lambda_q1_0, lambda_k1_0, lambda_q2_0, lambda_k2_0)]

    meta = jnp.broadcast_to(meta_tokens[None].astype(x_prompt.dtype), (B, N_META, D_MODEL))
    h_p = jnp.concatenate([meta, x_prompt], axis=1).reshape(B * L, D_MODEL)
    h_s = x_sample.reshape(Bd, D_MODEL)
    tabs_p = _rope_tables(jnp.arange(L, dtype=jnp.int32))
    past_len = page_table.shape[1] * PAGE
    tabs_s = _rope_tables(jnp.full((Bd,), past_len, dtype=jnp.int32))

    z4_p, k_p, v_p = _inproj0(h_p, row(pre_norm_0), w0, tabs_p, ROW_TILE, L // ROW_TILE)
    mix_a = _pool_prompt(z4_p, pw, row(pool_scale_0), ROW_TILE, L // ROW_TILE)
    mix_b = _attn_prompt(z4_p, k_p, v_p, lams, row(subln_0), B, L)
    h_p = _outproj([mix_a, mix_b], wo0, h_p, row(post_norm_0), ROW_TILE)

    z4_s, k_s, v_s = _inproj0(h_s, row(pre_norm_0), w0, tabs_s, Bd, 1)
    mix_a_s = _pool_sample(z4_s, state_pool.reshape(Bd, POOL_CTX * 1024), pw, row(pool_scale_0))
    hd = lambda a: a.reshape(Bd, DA_HEADS, LANES)
    mix_b_s = _attn_sample(page_table, hd(z4_s[:, 2048:3072]), hd(k_s), hd(v_s),
                           hd(z4_s[:, 3072:4096]), lams, row(subln_0), cache_k, cache_v)
    h_s = _outproj([mix_a_s, mix_b_s.reshape(Bd, 1024)], wo0, h_s, row(post_norm_0), Bd)

    z1_p, lr_p = _inproj1(h_p, row(pre_norm_1), w1, w1_lr, ROW_TILE)
    mix1_p, gla_p = _gla_prompt(z1_p, lr_p, wa, row(gla_ba_1), row(gla_norm_1), B, L)
    h_p = _outproj([mix1_p], wo1, h_p, row(post_norm_1), ROW_TILE)

    z1_s, lr_s = _inproj1(h_s, row(pre_norm_1), w1, w1_lr, Bd)
    mix1_s, gla_s = _gla_sample(z1_s.reshape(Bd, 1, 6144), lr_s.reshape(Bd, 1, LANES), wa,
                                row(gla_ba_1), row(gla_norm_1), state_gla)
    h_s = _outproj([mix1_s.reshape(Bd, 2048)], wo1, h_s, row(post_norm_1), Bd)

    u_p = z4_p[:, :1024].reshape(B, L, 1024)
    pool_p = u_p[:, L - POOL_CTX:]
    pool_s = jnp.concatenate([state_pool[:, 1:], z4_s[:, None, :1024]], axis=1)
    y_p = h_p.reshape(B, L, D_MODEL)[:, N_META:]
    return (y_p, h_s.reshape(Bd, 1, D_MODEL),
            k_p.reshape(B, L, DA_HEADS, LANES), v_p.reshape(B, L, DA_HEADS, LANES), pool_p, gla_p,
            k_s.reshape(Bd, 1, DA_HEADS, LANES), v_s.reshape(Bd, 1, DA_HEADS, LANES), pool_s, gla_s)
```

```python
import functools
import math

import numpy as np
import jax
import jax.numpy as jnp
from jax import lax
from jax.experimental import pallas as pl
from jax.experimental.pallas import tpu as pltpu

f32 = jnp.float32
bf16 = jnp.bfloat16

D_MODEL = 2048
N_META = 16
NORM_EPS = 1e-6
POOL_WINDOWS = (2, 4, 8, 16)
POOL_GROUP = 256
POOL_CTX = 15
DA_HEADS = 8
DA_VDIM = 128
DA_QKDIM = 64
DA_SCALE = DA_QKDIM ** -0.5
ROT_DIM = 16
ROPE_THETA = 500000.0
LAMBDA_INIT = 0.8 - 0.6 * math.exp(-0.3 * 0)
GLA_HEADS = 4
GLA_KDIM = 256
GLA_VDIM = 512
GLA_RANK = 16
GLA_TAU = 16.0
GLA_CHUNK = 64
GLA_UNROLL = 2
PAGE = 128

LANES = 128
ROW_TILE = 688
OUT_TILE = 512
ATTN_TQ = 512
ATTN_SUB = 8
LOG2E = 1.4426950408889634
VMEM_LIMIT = 52 * 1024 * 1024

NT_DIMS = (((1,), (1,)), ((), ()))
TN_DIMS = (((0,), (0,)), ((), ()))


def _cparams(sem):
    return pltpu.CompilerParams(dimension_semantics=sem, vmem_limit_bytes=VMEM_LIMIT)


def _sigmoid(x):
    return 1.0 / (1.0 + jnp.exp(-x))


def _rms_rows(x, g):
    ms = jnp.mean(x * x, axis=-1, keepdims=True)
    return x * lax.rsqrt(ms + NORM_EPS) * g


def _rope_store(dst_ref, a, c, s1, s2, scale):
    for h in range(DA_HEADS):
        xh = a[:, h * LANES:(h + 1) * LANES]
        r = xh * c + pltpu.roll(xh, 8, 1) * s1 + pltpu.roll(xh, LANES - 8, 1) * s2
        dst_ref[:, h * LANES:(h + 1) * LANES] = r * scale if scale != 1.0 else r


def _seq_tile_offset(i, tm, tiles_per_seq, seq_rows):
    r = i % tiles_per_seq
    return pl.multiple_of((i // tiles_per_seq) * seq_rows + jnp.maximum(r * tm - N_META, 0), 16)


def _inproj0_kernel(*refs, tiles_per_seq):
    if tiles_per_seq:
        x_ref, meta_ref, g_ref, w_ref, c_ref, s1_ref, s2_ref, z_ref, k_ref, v_ref, xn_ref = refs
    else:
        x_ref, g_ref, w_ref, c_ref, s1_ref, s2_ref, z_ref, k_ref, v_ref, xn_ref = refs
    j = pl.program_id(1)
    tm = xn_ref.shape[0]

    @pl.when(j == 0)
    def _():
        if tiles_per_seq:
            first = pl.program_id(0) % tiles_per_seq == 0

            @pl.when(first)
            def _():
                xn_ref[0:N_META, :] = _rms_rows(meta_ref[...], g_ref[...]).astype(bf16)
                xn_ref[N_META:, :] = _rms_rows(x_ref[0:tm - N_META, :], g_ref[...]).astype(bf16)

            @pl.when(jnp.logical_not(first))
            def _():
                xn_ref[...] = _rms_rows(x_ref[...], g_ref[...]).astype(bf16)
        else:
            xn_ref[...] = _rms_rows(x_ref[...], g_ref[...]).astype(bf16)

    acc = jnp.dot(xn_ref[...], w_ref[...], preferred_element_type=f32)

    @pl.when((j == 0) | (j == 1) | (j == 3))
    def _():
        z_ref[...] = acc

    @pl.when(j == 2)
    def _():
        _rope_store(z_ref, acc, c_ref[...], s1_ref[...], s2_ref[...], DA_SCALE)

    @pl.when(j == 4)
    def _():
        _rope_store(k_ref, acc, c_ref[...], s1_ref[...], s2_ref[...], 1.0)

    @pl.when(j == 5)
    def _():
        v_ref[...] = acc


def _inproj0(x, g, w, tabs, tm, tab_blocks, meta=None, n_seq=None):
    c, s1, s2 = tabs
    tab_spec = pl.BlockSpec((tm, LANES), lambda i, j: (i % tab_blocks, 0))
    if meta is None:
        m = x.shape[0]
        x_specs = [pl.BlockSpec((tm, D_MODEL), lambda i, j: (i, 0))]
        operands = (x,)
    else:
        seq_rows = x.shape[0] // n_seq
        m = x.shape[0] + n_seq * N_META
        x_specs = [
            pl.BlockSpec((pl.Element(tm), pl.Element(D_MODEL)),
                         lambda i, j: (_seq_tile_offset(i, tm, tab_blocks, seq_rows), 0)),
            pl.BlockSpec((N_META, D_MODEL), lambda i, j: (0, 0)),
        ]
        operands = (x, meta)
    return pl.pallas_call(
        functools.partial(_inproj0_kernel, tiles_per_seq=tab_blocks if meta is not None else 0),
        grid=(m // tm, 6),
        in_specs=x_specs + [
            pl.BlockSpec((1, D_MODEL), lambda i, j: (0, 0)),
            pl.BlockSpec((D_MODEL, 1024),
                         lambda i, j: (0, jnp.where(j < 3, j, jnp.where(j == 3, 5, j - 1)))),
            tab_spec, tab_spec, tab_spec,
        ],
        out_specs=[
            pl.BlockSpec((tm, 1024), lambda i, j: (i, jnp.minimum(j, 3))),
            pl.BlockSpec((tm, 1024), lambda i, j: (i, 0)),
            pl.BlockSpec((tm, 1024), lambda i, j: (i, 0)),
        ],
        out_shape=[
            jax.ShapeDtypeStruct((m, 4096), f32),
            jax.ShapeDtypeStruct((m, 1024), f32),
            jax.ShapeDtypeStruct((m, 1024), f32),
        ],
        scratch_shapes=[pltpu.VMEM((tm, D_MODEL), bf16)],
        compiler_params=_cparams(("arbitrary", "arbitrary")),
        name="inproj0",
    )(*operands, g, w, c, s1, s2)


def _inproj1_kernel(x_ref, g_ref, w_ref, wlr_ref, z_ref, lr_ref, xn_ref):
    j = pl.program_id(1)

    @pl.when(j == 0)
    def _():
        xn_ref[...] = _rms_rows(x_ref[...], g_ref[...]).astype(bf16)
        lr_ref[...] = jnp.dot(xn_ref[...], wlr_ref[...], preferred_element_type=f32)

    acc = jnp.dot(xn_ref[...], w_ref[...], preferred_element_type=f32)

    @pl.when(j == 0)
    def _():
        z_ref[...] = acc * (GLA_KDIM ** -0.5)

    @pl.when(j != 0)
    def _():
        z_ref[...] = acc


def _inproj1(x, g, w, wlr, tm):
    m = x.shape[0]
    return pl.pallas_call(
        _inproj1_kernel,
        grid=(m // tm, 6),
        in_specs=[
            pl.BlockSpec((tm, D_MODEL), lambda i, j: (i, 0)),
            pl.BlockSpec((1, D_MODEL), lambda i, j: (0, 0)),
            pl.BlockSpec((D_MODEL, 1024), lambda i, j: (0, j)),
            pl.BlockSpec((D_MODEL, LANES), lambda i, j: (0, 0)),
        ],
        out_specs=[
            pl.BlockSpec((tm, 1024), lambda i, j: (i, j)),
            pl.BlockSpec((tm, LANES), lambda i, j: (i, 0)),
        ],
        out_shape=[
            jax.ShapeDtypeStruct((m, 6144), f32),
            jax.ShapeDtypeStruct((m, LANES), f32),
        ],
        scratch_shapes=[pltpu.VMEM((tm, D_MODEL), bf16)],
        compiler_params=_cparams(("arbitrary", "arbitrary")),
        name="inproj1",
    )(x, g, w, wlr)


def _outproj_kernel(*refs, n_parts, tiles_per_seq):
    mix_refs = refs[:n_parts]
    if tiles_per_seq:
        w_ref, h_ref, meta_ref, g_ref, o_ref = refs[n_parts:]
    else:
        w_ref, h_ref, g_ref, o_ref = refs[n_parts:]
    kw = w_ref.shape[0] // n_parts
    y = None
    for p in range(n_parts):
        t = jnp.dot(mix_refs[p][...], w_ref[p * kw:(p + 1) * kw, :], preferred_element_type=f32)
        y = t if y is None else y + t
    rn = _rms_rows(y, g_ref[...])
    if tiles_per_seq:
        tm = o_ref.shape[0]
        first = pl.program_id(0) % tiles_per_seq == 0

        @pl.when(first)
        def _():
            o_ref[0:N_META, :] = meta_ref[...] + rn[0:N_META]
            o_ref[N_META:, :] = h_ref[0:tm - N_META, :] + rn[N_META:]

        @pl.when(jnp.logical_not(first))
        def _():
            o_ref[...] = h_ref[...] + rn
    else:
        o_ref[...] = h_ref[...] + rn


def _outproj(mix_parts, w, h, g, tm, *, n_tiles=None, row_offset=None, meta=None, tiles_per_seq=0,
             seq_rows=None):
    n_parts = len(mix_parts)
    kw = w.shape[0] // n_parts
    if n_tiles is None:
        n_tiles = mix_parts[0].shape[0] // tm
    if row_offset is not None:
        mix_spec = pl.BlockSpec((pl.Element(tm), pl.Element(kw)), lambda i: (row_offset(i), 0))
        h_specs = [pl.BlockSpec((pl.Element(tm), pl.Element(D_MODEL)),
                                lambda i: (row_offset(i), 0))]
    else:
        mix_spec = pl.BlockSpec((tm, kw), lambda i: (i, 0))
        h_specs = [pl.BlockSpec((tm, D_MODEL), lambda i: (i, 0))]
    operands = [h]
    if meta is not None:
        h_specs = [
            pl.BlockSpec((pl.Element(tm), pl.Element(D_MODEL)),
                         lambda i: (_seq_tile_offset(i, tm, tiles_per_seq, seq_rows), 0)),
            pl.BlockSpec((N_META, D_MODEL), lambda i: (0, 0)),
        ]
        operands = [h, meta]
    return pl.pallas_call(
        functools.partial(_outproj_kernel, n_parts=n_parts, tiles_per_seq=tiles_per_seq),
        grid=(n_tiles,),
        in_specs=[mix_spec] * n_parts
                 + [pl.BlockSpec(w.shape, lambda i: (0, 0), pipeline_mode=pl.Buffered(1))]
                 + h_specs + [pl.BlockSpec((1, D_MODEL), lambda i: (0, 0))],
        out_specs=pl.BlockSpec((tm, D_MODEL), lambda i: (i, 0)),
        out_shape=jax.ShapeDtypeStruct((n_tiles * tm, D_MODEL), f32),
        compiler_params=_cparams(("arbitrary",)),
        name="outproj",
    )(*mix_parts, w, *operands, g)


def _pool_finish(window_sum, u, cnt, pw, scale, gate):
    d = window_sum / cnt - u
    po = jnp.dot(d.astype(bf16), pw, preferred_element_type=f32) * scale
    return (po * (gate * _sigmoid(gate))).astype(bf16)


def _pool_prompt_kernel(u_ref, halo_ref, pg_ref, pw_ref, sc_ref, o_ref, ext_ref, *, tiles_per_seq):
    tm = u_ref.shape[0]
    r = pl.program_id(0) % tiles_per_seq
    halo = halo_ref[...]
    ext_ref[0:16, :] = jnp.where(r == 0, jnp.zeros_like(halo), halo)
    ext_ref[16:, :] = u_ref[...]
    pos = r * tm + lax.broadcasted_iota(jnp.int32, (tm, 1), 0)
    for gi, w in enumerate(POOL_WINDOWS):
        cs = slice(gi * POOL_GROUP, (gi + 1) * POOL_GROUP)
        u = ext_ref[16:16 + tm, cs]
        s = u
        for t in range(1, w):
            s = s + ext_ref[16 - t:16 - t + tm, cs]
        cnt = jnp.minimum(w, pos + 1).astype(f32)
        o_ref[:, cs] = _pool_finish(s, u, cnt, pw_ref[gi], sc_ref[:, cs], pg_ref[:, cs])


def _pool_prompt(z4, pool_w, pool_scale, tm, tiles_per_seq):
    m = z4.shape[0]
    hb = tm // 16
    return pl.pallas_call(
        functools.partial(_pool_prompt_kernel, tiles_per_seq=tiles_per_seq),
        grid=(m // tm,),
        in_specs=[
            pl.BlockSpec((tm, 1024), lambda i: (i, 0)),
            pl.BlockSpec((16, 1024), lambda i: (jnp.maximum(i * hb - 1, 0), 0)),
            pl.BlockSpec((tm, 1024), lambda i: (i, 1)),
            pl.BlockSpec((4, POOL_GROUP, POOL_GROUP), lambda i: (0, 0, 0)),
            pl.BlockSpec((1, 1024), lambda i: (0, 0)),
        ],
        out_specs=pl.BlockSpec((tm, 1024), lambda i: (i, 0)),
        out_shape=jax.ShapeDtypeStruct((m, 1024), bf16),
        scratch_shapes=[pltpu.VMEM((tm + 16, 1024), f32)],
        compiler_params=_cparams(("arbitrary",)),
        name="pool_prompt",
    )(z4, z4, z4, pool_w, pool_scale)


def _pool_sample_kernel(u_ref, prev_ref, pg_ref, pw_ref, sc_ref, o_ref):
    for gi, w in enumerate(POOL_WINDOWS):
        cs = slice(gi * POOL_GROUP, (gi + 1) * POOL_GROUP)
        u = u_ref[:, cs]
        s = u
        for t in range(1, w):
            base = (POOL_CTX - t) * 1024 + gi * POOL_GROUP
            s = s + prev_ref[:, base:base + POOL_GROUP]
        o_ref[:, cs] = _pool_finish(s, u, float(w), pw_ref[gi], sc_ref[:, cs], pg_ref[:, cs])


def _pool_sample(z4, prev_flat, pool_w, pool_scale):
    m = z4.shape[0]
    return pl.pallas_call(
        _pool_sample_kernel,
        grid=(1,),
        in_specs=[
            pl.BlockSpec((m, 1024), lambda i: (0, 0)),
            pl.BlockSpec(prev_flat.shape, lambda i: (0, 0)),
            pl.BlockSpec((m, 1024), lambda i: (0, 1)),
            pl.BlockSpec((4, POOL_GROUP, POOL_GROUP), lambda i: (0, 0, 0)),
            pl.BlockSpec((1, 1024), lambda i: (0, 0)),
        ],
        out_specs=pl.BlockSpec((m, 1024), lambda i: (0, 0)),
        out_shape=jax.ShapeDtypeStruct((m, 1024), bf16),
        compiler_params=_cparams(("arbitrary",)),
        name="pool_sample",
    )(z4, prev_flat, z4, pool_w, pool_scale)


def _lambda(lq1, lk1, lq2, lk2):
    a = jnp.sum(lq1[...] * lk1[...], axis=-1, keepdims=True)
    b = jnp.sum(lq2[...] * lk2[...], axis=-1, keepdims=True)
    return jnp.exp(a) - jnp.exp(b) + LAMBDA_INIT


def _attn_finish(o, sub, gate):
    on = _rms_rows(o, sub) * (1.0 - LAMBDA_INIT)
    return (on * (gate * _sigmoid(gate))).astype(bf16)


def _attn_prompt_kernel(q_ref, k_ref, v_ref, ag_ref, lq1, lk1, lq2, lk2, sub_ref, o_ref,
                        kb_ref, vb_ref):
    kb_ref[...] = k_ref[...].astype(bf16)
    vb_ref[...] = v_ref[...].astype(bf16)
    lam = _lambda(lq1, lk1, lq2, lk2)
    first_map = lax.broadcasted_iota(jnp.int32, (1, LANES), 1) < DA_QKDIM
    n_tiles = (q_ref.shape[0] - N_META) // ATTN_TQ

    def stack_maps(q):
        q = q * LOG2E
        return jnp.concatenate([jnp.where(first_map, q, 0.0), jnp.where(first_map, 0.0, q)],
                               axis=0).astype(bf16)

    def causal(t):
        row = lax.broadcasted_iota(jnp.int32, (2 * t, t), 0)
        col = lax.broadcasted_iota(jnp.int32, (2 * t, t), 1)
        return jnp.where(row >= t, row - t, row) >= col

    def step(qq, carry, kt, vt, mask):
        m, l, acc = carry
        s = lax.dot_general(qq, kt, NT_DIMS, preferred_element_type=f32)
        if mask is not None:
            s = jnp.where(mask, s, -jnp.inf)
        m_new = jnp.maximum(m, jnp.max(s, axis=-1, keepdims=True))
        alpha = jnp.exp2(m - m_new)
        p = jnp.exp2(s - m_new)
        l = alpha * l + jnp.sum(p, axis=-1, keepdims=True)
        acc = alpha * acc + jnp.dot(p.astype(bf16), vt, preferred_element_type=f32)
        return m_new, l, acc

    def init(t):
        return (jnp.full((2 * t, 1), -jnp.inf, f32), jnp.zeros((2 * t, 1), f32),
                jnp.zeros((2 * t, LANES), f32))

    def finish(carry, t, rows):
        _, l, acc = carry
        o = acc[:t] / l[:t] - lam * (acc[t:] / l[t:])
        o_ref[rows, :] = _attn_finish(o, sub_ref[...], ag_ref[rows, :])

    meta = slice(0, N_META)
    qq = stack_maps(q_ref[meta, :])
    carry = step(qq, init(N_META), kb_ref[meta, :], vb_ref[meta, :], causal(N_META))
    finish(carry, N_META, meta)

    diag_mask = causal(ATTN_TQ)

    def q_tile(i, _):
        r0 = pl.multiple_of(N_META + i * ATTN_TQ, 16)
        rows = pl.ds(r0, ATTN_TQ)
        qq = stack_maps(q_ref[rows, :])
        carry = step(qq, init(ATTN_TQ), kb_ref[meta, :], vb_ref[meta, :], None)

        def kv_tile(j, c):
            cols = pl.ds(pl.multiple_of(N_META + j * ATTN_TQ, 16), ATTN_TQ)
            return step(qq, c, kb_ref[cols, :], vb_ref[cols, :], None)

        carry = lax.fori_loop(0, i, kv_tile, carry)
        carry = step(qq, carry, kb_ref[rows, :], vb_ref[rows, :], diag_mask)
        finish(carry, ATTN_TQ, rows)
        return 0

    lax.fori_loop(0, n_tiles, q_tile, 0)


def _attn_prompt(z4, k, v, lams, subln, n_seq, seq_len):
    m = z4.shape[0]
    vec = pl.BlockSpec((1, DA_QKDIM), lambda b, h: (0, 0))
    return pl.pallas_call(
        _attn_prompt_kernel,
        grid=(n_seq, DA_HEADS),
        in_specs=[
            pl.BlockSpec((seq_len, LANES), lambda b, h: (b, 16 + h)),
            pl.BlockSpec((seq_len, LANES), lambda b, h: (b, h)),
            pl.BlockSpec((seq_len, LANES), lambda b, h: (b, h)),
            pl.BlockSpec((seq_len, LANES), lambda b, h: (b, 24 + h)),
            vec, vec, vec, vec,
            pl.BlockSpec((1, LANES), lambda b, h: (0, 0)),
        ],
        out_specs=pl.BlockSpec((seq_len, LANES), lambda b, h: (b, h)),
        out_shape=jax.ShapeDtypeStruct((m, 1024), bf16),
        scratch_shapes=[pltpu.VMEM((seq_len, LANES), bf16), pltpu.VMEM((seq_len, LANES), bf16)],
        compiler_params=_cparams(("arbitrary", "arbitrary")),
        name="attn_prompt",
    )(z4, k, v, z4, *lams, subln)


def _half_sum_matrix():
    d = np.arange(LANES)[:, None] < DA_QKDIM
    c = np.arange(LANES)[None, :] < DA_QKDIM
    return (d == c).astype(np.float32)


def _attn_sample_kernel(pt_ref, q_ref, kn_ref, vn_ref, ag_ref, lq1, lk1, lq2, lk2, sub_ref, e_ref,
                        *rest, pages):
    del pt_ref
    kp_refs, vp_refs = rest[:pages], rest[pages:2 * pages]
    o_ref, m_ref, l_ref, acc_ref, accr_ref = rest[2 * pages:]
    p = pl.program_id(1)
    q = q_ref[0] * LOG2E
    first_half = lax.broadcasted_iota(jnp.int32, (DA_HEADS, LANES), 1) < DA_QKDIM
    swap = lambda a: pltpu.roll(a, DA_QKDIM, a.ndim - 1)

    @pl.when(p == 0)
    def _():
        prod = kn_ref[0] * q
        s0 = jnp.sum(jnp.where(first_half, prod, 0.0), axis=-1, keepdims=True)
        s1 = jnp.sum(jnp.where(first_half, 0.0, prod), axis=-1, keepdims=True)
        m_ref[...] = jnp.where(first_half, s0, s1)
        l_ref[...] = jnp.ones_like(l_ref)
        acc_ref[...] = vn_ref[0]
        accr_ref[...] = swap(vn_ref[0])

    m, l, acc, acc_r = m_ref[...], l_ref[...], acc_ref[...], accr_ref[...]
    for pg in range(pages):
        for st in range(PAGE // ATTN_SUB):
            rows = slice(st * ATTN_SUB, (st + 1) * ATTN_SUB)
            kp = kp_refs[pg][0, rows]
            vp = vp_refs[pg][0, rows]
            prod = (kp * q[None]).reshape(ATTN_SUB * DA_HEADS, LANES).astype(bf16)
            r = jnp.dot(prod, e_ref[...], preferred_element_type=f32)
            r = r.reshape(ATTN_SUB, DA_HEADS, LANES)
            m_new = jnp.maximum(m, jnp.max(r, axis=0))
            alpha = jnp.exp2(m - m_new)
            pw = jnp.exp2(r - m_new[None])
            l = alpha * l + jnp.sum(pw, axis=0)
            acc = alpha * acc + jnp.sum(pw * vp, axis=0)
            acc_r = alpha * acc_r + jnp.sum(pw * swap(vp), axis=0)
            m = m_new
    m_ref[...], l_ref[...], acc_ref[...], accr_ref[...] = m, l, acc, acc_r

    @pl.when(p == pl.num_programs(1) - 1)
    def _():
        lam = _lambda(lq1, lk1, lq2, lk2)
        other = swap(acc_r)
        l_sw = swap(l)
        o0 = jnp.where(first_half, acc, other) / jnp.where(first_half, l, l_sw)
        o1 = jnp.where(first_half, other, acc) / jnp.where(first_half, l_sw, l)
        o_ref[0] = _attn_finish(o0 - lam * o1, sub_ref[...], ag_ref[0])


def _attn_sample(page_table, q, k_new, v_new, ag, lams, subln, cache_k, cache_v, pages=8):
    bd, n_pages = page_table.shape
    assert n_pages % pages == 0
    tok = pl.BlockSpec((1, DA_HEADS, LANES), lambda b, p, pt: (b, 0, 0))
    vec = pl.BlockSpec((1, DA_QKDIM), lambda b, p, pt: (0, 0))

    def page(i):
        return pl.BlockSpec((1, PAGE, DA_HEADS, LANES),
                            lambda b, p, pt: (pt[b * n_pages + p * pages + i], 0, 0, 0))

    e = jnp.asarray(_half_sum_matrix(), dtype=bf16)
    grid_spec = pltpu.PrefetchScalarGridSpec(
        num_scalar_prefetch=1,
        grid=(bd, n_pages // pages),
        in_specs=[tok, tok, tok, tok, vec, vec, vec, vec,
                  pl.BlockSpec((1, LANES), lambda b, p, pt: (0, 0)),
                  pl.BlockSpec((LANES, LANES), lambda b, p, pt: (0, 0))]
                 + [page(i) for i in range(pages)] * 2,
        out_specs=tok,
        scratch_shapes=[pltpu.VMEM((DA_HEADS, LANES), f32) for _ in range(4)],
    )
    return pl.pallas_call(
        functools.partial(_attn_sample_kernel, pages=pages),
        grid_spec=grid_spec,
        out_shape=jax.ShapeDtypeStruct((bd, DA_HEADS, LANES), bf16),
        compiler_params=_cparams(("arbitrary", "arbitrary")),
        name="attn_sample",
    )(page_table.reshape(-1), q, k_new, v_new, ag, *lams, subln, e,
      *([cache_k] * pages), *([cache_v] * pages))


def _gla_consts(c):
    n = int(round(math.log2(c)))
    t = np.arange(c)[:, None]
    u = np.arange(c)[None, :]
    sums = [(u <= t), (u > t)]
    masks = [(u == t)]
    for lvl in range(n):
        half = c >> (lvl + 1)
        par = 2 * half
        split = (t // par) * par + half - 1
        upper = (t % par) >= half
        sums.append((upper & (u > split) & (u <= t)) | ((~upper) & (u > t) & (u <= split)))
        masks.append(upper & ((u % par) < half) & ((u // par) == (t // par)))
    sums = np.stack(sums).astype(np.float32).reshape(-1, c)
    return np.concatenate([sums] * 3, axis=1), np.stack(masks).astype(np.float32)


def _split3(x):
    hi = x.astype(bf16)
    r = x - hi.astype(f32)
    mid = r.astype(bf16)
    lo = (r - mid.astype(f32)).astype(bf16)
    return jnp.concatenate([hi, mid, lo], axis=0)


def _log_sigmoid(x):
    return jnp.minimum(x, 0.0) - jnp.log1p(jnp.exp(-jnp.abs(x)))


def _log_decay(lr, wa, ba):
    x = jnp.dot(lr.astype(bf16), wa, preferred_element_type=f32) + ba
    return _log_sigmoid(x) * (1.0 / GLA_TAU)


def _gla_finish(o, gn, gate):
    return (_rms_rows(o, gn) * (gate * _sigmoid(gate))).astype(bf16)


def _gla_prompt_kernel(q_ref, k_ref, v_ref, gate_ref, lr_ref, wa_ref, ba_ref, gn_ref,
                       sum16_ref, msk16_ref, sum64_ref, msk64_ref, o_ref, st_ref, g_ref, s_ref):
    g_ref[...] = _log_decay(lr_ref[...], wa_ref[...], ba_ref[...]) * LOG2E

    def chunk(st, r0, c, sum_ref, msk_ref):
        n = msk_ref.shape[0] - 1
        rows = pl.ds(r0, c)
        q = q_ref[rows, :]
        k = k_ref[rows, :]
        vb = v_ref[rows, :].astype(bf16)
        e = jnp.exp2(jnp.dot(sum_ref[...], _split3(g_ref[rows, :]),
                             preferred_element_type=f32))
        o = lax.dot_general((q * e[0:c]).astype(bf16), st.astype(bf16), NT_DIMS,
                            preferred_element_type=f32)
        sc = msk_ref[0] * lax.dot_general(q.astype(bf16), k.astype(bf16), NT_DIMS,
                                          preferred_element_type=f32)
        for lvl in range(n):
            el = e[(2 + lvl) * c:(3 + lvl) * c]
            sc = sc + msk_ref[1 + lvl] * lax.dot_general(
                (q * el).astype(bf16), (k * el).astype(bf16), NT_DIMS, preferred_element_type=f32)
        o = o + jnp.dot(sc.astype(bf16), vb, preferred_element_type=f32)
        k_end = (k * e[c:2 * c]).astype(bf16)
        o_ref[rows, :] = _gla_finish(o, gn_ref[...], gate_ref[rows, :])
        return st * e[c - 1:c] + lax.dot_general(vb, k_end, TN_DIMS, preferred_element_type=f32)

    s_ref[...] = chunk(jnp.zeros(s_ref.shape, f32), 0, N_META, sum16_ref, msk16_ref)

    def body(ci, _):
        st = s_ref[...]
        for u in range(GLA_UNROLL):
            r0 = pl.multiple_of(N_META + (ci * GLA_UNROLL + u) * GLA_CHUNK, 16)
            st = chunk(st, r0, GLA_CHUNK, sum64_ref, msk64_ref)
        s_ref[...] = st
        return 0

    n_chunks = (q_ref.shape[0] - N_META) // GLA_CHUNK
    assert n_chunks % GLA_UNROLL == 0
    lax.fori_loop(0, n_chunks // GLA_UNROLL, body, 0)
    st_ref[0, 0] = s_ref[...].T


def _gla_prompt(z1, lr, wa, ba, gn, n_seq, seq_len):
    m = z1.shape[0]
    sum16, msk16 = _gla_consts(N_META)
    sum64, msk64 = _gla_consts(GLA_CHUNK)
    consts = [jnp.asarray(sum16, bf16), jnp.asarray(msk16), jnp.asarray(sum64, bf16),
              jnp.asarray(msk64)]

    def full(a):
        nd = a.ndim
        return pl.BlockSpec(a.shape, lambda b, h: (0,) * nd)

    return pl.pallas_call(
        _gla_prompt_kernel,
        grid=(n_seq, GLA_HEADS),
        in_specs=[
            pl.BlockSpec((seq_len, GLA_KDIM), lambda b, h: (b, h)),
            pl.BlockSpec((seq_len, GLA_KDIM), lambda b, h: (b, 4 + h)),
            pl.BlockSpec((seq_len, GLA_VDIM), lambda b, h: (b, 4 + h)),
            pl.BlockSpec((seq_len, GLA_VDIM), lambda b, h: (b, 8 + h)),
            pl.BlockSpec((seq_len, LANES), lambda b, h: (b, 0)),
            pl.BlockSpec((LANES, GLA_KDIM), lambda b, h: (0, h)),
            pl.BlockSpec((1, GLA_KDIM), lambda b, h: (0, h)),
            pl.BlockSpec((1, GLA_VDIM), lambda b, h: (0, 0)),
        ] + [full(a) for a in consts],
        out_specs=[
            pl.BlockSpec((seq_len, GLA_VDIM), lambda b, h: (b, h)),
            pl.BlockSpec((1, 1, GLA_KDIM, GLA_VDIM), lambda b, h: (b, h, 0, 0)),
        ],
        out_shape=[
            jax.ShapeDtypeStruct((m, GLA_HEADS * GLA_VDIM), bf16),
            jax.ShapeDtypeStruct((n_seq, GLA_HEADS, GLA_KDIM, GLA_VDIM), f32),
        ],
        scratch_shapes=[pltpu.VMEM((seq_len, GLA_KDIM), f32), pltpu.VMEM((GLA_VDIM, GLA_KDIM), f32)],
        compiler_params=_cparams(("arbitrary", "arbitrary")),
        name="gla_prompt",
    )(z1, z1, z1, z1, lr, wa, ba, gn, *consts)


def _gla_sample_kernel(z_ref, lr_ref, wa_ref, wat_ref, ba_ref, bat_ref, gn_ref, s_ref,
                       o_ref, st_ref):
    lr = lr_ref[0].astype(bf16)
    lr_rows = jnp.broadcast_to(lr, (LANES, LANES))
    first_row = lax.broadcasted_iota(jnp.int32, (16, 1), 0) == 0
    kw, vw = GLA_HEADS * GLA_KDIM, GLA_HEADS * GLA_VDIM
    for h in range(GLA_HEADS):
        kc = slice(h * GLA_KDIM, (h + 1) * GLA_KDIM)
        vc = slice(h * GLA_VDIM, (h + 1) * GLA_VDIM)
        q = z_ref[0, :, kc]
        k = z_ref[0, :, kw + h * GLA_KDIM:kw + (h + 1) * GLA_KDIM]
        v = z_ref[0, :, 2 * kw + h * GLA_VDIM:2 * kw + (h + 1) * GLA_VDIM]
        gate = z_ref[0, :, 2 * kw + vw + h * GLA_VDIM:2 * kw + vw + (h + 1) * GLA_VDIM]
        x_row = jnp.dot(lr, wa_ref[:, kc], preferred_element_type=f32) + ba_ref[:, kc]
        x_col = lax.dot_general(wat_ref[kc, :], lr_rows, NT_DIMS,
                                preferred_element_type=f32) + bat_ref[kc, :]
        eg_row = jnp.exp(_log_sigmoid(x_row) * (1.0 / GLA_TAU))
        eg_col = jnp.exp(_log_sigmoid(x_col) * (1.0 / GLA_TAU))
        s = s_ref[0, h]
        o = jnp.dot((q * eg_row).astype(bf16), s.astype(bf16), preferred_element_type=f32)
        o = o + jnp.sum(q * k, axis=-1, keepdims=True) * v
        k16 = jnp.where(first_row, k, 0.0).astype(bf16)
        v16 = jnp.broadcast_to(v, (16, GLA_VDIM)).astype(bf16)
        st_ref[0, h] = (s * jnp.concatenate([eg_col] * (GLA_VDIM // LANES), axis=1)
                        + lax.dot_general(k16, v16, TN_DIMS, preferred_element_type=f32))
        o_ref[0, :, vc] = _gla_finish(o, gn_ref[...], gate)


def _gla_sample(z1, lr, wa, ba, gn, state):
    bd = state.shape[0]
    kw = GLA_HEADS * GLA_KDIM
    wat = wa.T
    bat = jnp.broadcast_to(ba.reshape(kw, 1), (kw, LANES))
    const = lambda a: pl.BlockSpec(a.shape, lambda b: (0,) * a.ndim)
    st_spec = pl.BlockSpec((1, GLA_HEADS, GLA_KDIM, GLA_VDIM), lambda b: (b, 0, 0, 0))
    return pl.pallas_call(
        _gla_sample_kernel,
        grid=(bd,),
        in_specs=[
            pl.BlockSpec((1, 1, z1.shape[2]), lambda b: (b, 0, 0)),
            pl.BlockSpec((1, 1, LANES), lambda b: (b, 0, 0)),
            const(wa), const(wat), const(ba), const(bat), const(gn), st_spec,
        ],
        out_specs=[
            pl.BlockSpec((1, 1, GLA_HEADS * GLA_VDIM), lambda b: (b, 0, 0)),
            st_spec,
        ],
        out_shape=[
            jax.ShapeDtypeStruct((bd, 1, GLA_HEADS * GLA_VDIM), bf16),
            jax.ShapeDtypeStruct(state.shape, f32),
        ],
        compiler_params=_cparams(("arbitrary",)),
        name="gla_sample",
    )(z1, lr, wa, wat, ba, bat, gn, state)


def _rope_tables(positions):
    half = ROT_DIM // 2
    inv = jnp.power(ROPE_THETA, -jnp.arange(half, dtype=f32) * 2.0 / ROT_DIM)
    ang = positions.astype(f32)[:, None] * inv[None, :]
    cos, sin = jnp.cos(ang), jnp.sin(ang)
    t = positions.shape[0]
    rest = DA_QKDIM - ROT_DIM
    z8 = jnp.zeros((t, half), f32)
    zr = jnp.zeros((t, rest), f32)
    c = jnp.concatenate([cos, cos, jnp.ones((t, rest), f32)], axis=1)
    s1 = jnp.concatenate([z8, sin, zr], axis=1)
    s2 = jnp.concatenate([-sin, z8, zr], axis=1)
    return tuple(jnp.concatenate([a, a], axis=1) for a in (c, s1, s2))


def kernel(x_prompt, x_sample, cache_k, cache_v, state_pool, state_gla, page_table, meta_tokens,
           pre_norm_0, post_norm_0, w_in_0, pool_w_0, pool_scale_0, lambda_q1_0, lambda_k1_0,
           lambda_q2_0, lambda_k2_0, subln_0, w_out_0,
           pre_norm_1, post_norm_1, w_in_1, gla_wa2_1, gla_ba_1, gla_norm_1, w_out_1):
    B, seq = x_prompt.shape[0], x_prompt.shape[1]
    Bd = x_sample.shape[0]
    L = seq + N_META
    assert L % ROW_TILE == 0 and (L - N_META) % ATTN_TQ == 0 and x_sample.shape[1] == 1

    row = lambda a: a.reshape(1, -1)
    w0 = w_in_0.astype(bf16)
    w1 = w_in_1.astype(bf16)
    w1_lr = jnp.pad(w_in_1[:, 6144:], ((0, 0), (0, LANES - GLA_RANK))).astype(bf16)
    wa = jnp.pad(gla_wa2_1, ((0, LANES - GLA_RANK), (0, 0))).astype(bf16)
    wo0 = w_out_0.astype(bf16)
    wo1 = w_out_1.astype(bf16)
    pw = pool_w_0.astype(bf16)
    lams = [row(a) for a in (lambda_q1_0, lambda_k1_0, lambda_q2_0, lambda_k2_0)]

    meta = meta_tokens.astype(x_prompt.dtype)
    x_p = x_prompt.reshape(B * seq, D_MODEL)
    h_s = x_sample.reshape(Bd, D_MODEL)
    tiles = L // ROW_TILE
    tabs_p = _rope_tables(jnp.arange(L, dtype=jnp.int32))
    past_len = page_table.shape[1] * PAGE
    tabs_s = _rope_tables(jnp.full((Bd,), past_len, dtype=jnp.int32))

    z4_p, k_p, v_p = _inproj0(x_p, row(pre_norm_0), w0, tabs_p, ROW_TILE, tiles, meta=meta, n_seq=B)
    mix_a = _pool_prompt(z4_p, pw, row(pool_scale_0), ROW_TILE, tiles)
    mix_b = _attn_prompt(z4_p, k_p, v_p, lams, row(subln_0), B, L)
    h_p = _outproj([mix_a, mix_b], wo0, x_p, row(post_norm_0), ROW_TILE, meta=meta,
                   tiles_per_seq=tiles, seq_rows=seq)

    z4_s, k_s, v_s = _inproj0(h_s, row(pre_norm_0), w0, tabs_s, Bd, 1)
    mix_a_s = _pool_sample(z4_s, state_pool.reshape(Bd, POOL_CTX * 1024), pw, row(pool_scale_0))
    hd = lambda a: a.reshape(Bd, DA_HEADS, LANES)
    mix_b_s = _attn_sample(page_table, hd(z4_s[:, 2048:3072]), hd(k_s), hd(v_s),
                           hd(z4_s[:, 3072:4096]), lams, row(subln_0), cache_k, cache_v)
    h_s = _outproj([mix_a_s, mix_b_s.reshape(Bd, 1024)], wo0, h_s, row(post_norm_0), Bd)

    z1_p, lr_p = _inproj1(h_p, row(pre_norm_1), w1, w1_lr, ROW_TILE)
    mix1_p, gla_p = _gla_prompt(z1_p, lr_p, wa, row(gla_ba_1), row(gla_norm_1), B, L)
    per_seq = seq // OUT_TILE
    y_p = _outproj([mix1_p], wo1, h_p, row(post_norm_1), OUT_TILE, n_tiles=B * per_seq,
                   row_offset=lambda i: pl.multiple_of(
                       (i // per_seq) * L + N_META + (i % per_seq) * OUT_TILE, 16))

    z1_s, lr_s = _inproj1(h_s, row(pre_norm_1), w1, w1_lr, Bd)
    mix1_s, gla_s = _gla_sample(z1_s.reshape(Bd, 1, 6144), lr_s.reshape(Bd, 1, LANES), wa,
                                row(gla_ba_1), row(gla_norm_1), state_gla)
    h_s = _outproj([mix1_s.reshape(Bd, 2048)], wo1, h_s, row(post_norm_1), Bd)

    u_p = z4_p[:, :1024].reshape(B, L, 1024)
    pool_p = u_p[:, L - POOL_CTX:]
    pool_s = jnp.concatenate([state_pool[:, 1:], z4_s[:, None, :1024]], axis=1)
    return (y_p.reshape(B, seq, D_MODEL), h_s.reshape(Bd, 1, D_MODEL),
            k_p.reshape(B, L, DA_HEADS, LANES), v_p.reshape(B, L, DA_HEADS, LANES), pool_p, gla_p,
            k_s.reshape(Bd, 1, DA_HEADS, LANES), v_s.reshape(Bd, 1, DA_HEADS, LANES), pool_s, gla_s)
```

```python
import functools
import math

import numpy as np
import jax
import jax.numpy as jnp
from jax import lax
from jax.experimental import pallas as pl
from jax.experimental.pallas import tpu as pltpu

f32 = jnp.float32
bf16 = jnp.bfloat16

D_MODEL = 2048
N_META = 16
NORM_EPS = 1e-6
POOL_WINDOWS = (2, 4, 8, 16)
POOL_GROUP = 256
POOL_CTX = 15
DA_HEADS = 8
DA_VDIM = 128
DA_QKDIM = 64
DA_SCALE = DA_QKDIM ** -0.5
ROT_DIM = 16
ROPE_THETA = 500000.0
LAMBDA_INIT = 0.8 - 0.6 * math.exp(-0.3 * 0)
GLA_HEADS = 4
GLA_KDIM = 256
GLA_VDIM = 512
GLA_RANK = 16
GLA_TAU = 16.0
GLA_CHUNK = 64
GLA_UNROLL = 2
PAGE = 128

LANES = 128
ROW_TILE = 688
OUT_TILE = 512
ATTN_TQ = 512
ATTN_SUB = 8
LOG2E = 1.4426950408889634
VMEM_LIMIT = 52 * 1024 * 1024

NT_DIMS = (((1,), (1,)), ((), ()))
TN_DIMS = (((0,), (0,)), ((), ()))


def _cparams(sem):
    return pltpu.CompilerParams(dimension_semantics=sem, vmem_limit_bytes=VMEM_LIMIT)


def _sigmoid(x):
    return 1.0 / (1.0 + jnp.exp(-x))


def _rms_rows(x, g):
    ms = jnp.mean(x * x, axis=-1, keepdims=True)
    return x * lax.rsqrt(ms + NORM_EPS) * g


def _rope_store(dst_ref, a, c, s1, s2, scale):
    for h in range(DA_HEADS):
        xh = a[:, h * LANES:(h + 1) * LANES]
        r = xh * c + pltpu.roll(xh, 8, 1) * s1 + pltpu.roll(xh, LANES - 8, 1) * s2
        dst_ref[:, h * LANES:(h + 1) * LANES] = r * scale if scale != 1.0 else r


def _seq_tile_offset(i, tm, tiles_per_seq, seq_rows):
    r = i % tiles_per_seq
    return pl.multiple_of((i // tiles_per_seq) * seq_rows + jnp.maximum(r * tm - N_META, 0), 16)


def _inproj0_kernel(*refs, tiles_per_seq):
    if tiles_per_seq:
        x_ref, meta_ref, g_ref, w_ref, c_ref, s1_ref, s2_ref, z_ref, k_ref, v_ref, xn_ref = refs
    else:
        x_ref, g_ref, w_ref, c_ref, s1_ref, s2_ref, z_ref, k_ref, v_ref, xn_ref = refs
    j = pl.program_id(1)
    tm = xn_ref.shape[0]

    @pl.when(j == 0)
    def _():
        if tiles_per_seq:
            first = pl.program_id(0) % tiles_per_seq == 0

            @pl.when(first)
            def _():
                xn_ref[0:N_META, :] = _rms_rows(meta_ref[...], g_ref[...]).astype(bf16)
                xn_ref[N_META:, :] = _rms_rows(x_ref[0:tm - N_META, :], g_ref[...]).astype(bf16)

            @pl.when(jnp.logical_not(first))
            def _():
                xn_ref[...] = _rms_rows(x_ref[...], g_ref[...]).astype(bf16)
        else:
            xn_ref[...] = _rms_rows(x_ref[...], g_ref[...]).astype(bf16)

    acc = jnp.dot(xn_ref[...], w_ref[...], preferred_element_type=f32)

    @pl.when((j == 0) | (j == 1) | (j == 3))
    def _():
        z_ref[...] = acc

    @pl.when(j == 2)
    def _():
        _rope_store(z_ref, acc, c_ref[...], s1_ref[...], s2_ref[...], DA_SCALE)

    @pl.when(j == 4)
    def _():
        _rope_store(k_ref, acc, c_ref[...], s1_ref[...], s2_ref[...], 1.0)

    @pl.when(j == 5)
    def _():
        v_ref[...] = acc


def _inproj0(x, g, w, tabs, tm, tab_blocks, meta=None, n_seq=None):
    c, s1, s2 = tabs
    tab_spec = pl.BlockSpec((tm, LANES), lambda i, j: (i % tab_blocks, 0))
    if meta is None:
        m = x.shape[0]
        x_specs = [pl.BlockSpec((tm, D_MODEL), lambda i, j: (i, 0))]
        operands = (x,)
    else:
        seq_rows = x.shape[0] // n_seq
        m = x.shape[0] + n_seq * N_META
        x_specs = [
            pl.BlockSpec((pl.Element(tm), pl.Element(D_MODEL)),
                         lambda i, j: (_seq_tile_offset(i, tm, tab_blocks, seq_rows), 0)),
            pl.BlockSpec((N_META, D_MODEL), lambda i, j: (0, 0)),
        ]
        operands = (x, meta)
    return pl.pallas_call(
        functools.partial(_inproj0_kernel, tiles_per_seq=tab_blocks if meta is not None else 0),
        grid=(m // tm, 6),
        in_specs=x_specs + [
            pl.BlockSpec((1, D_MODEL), lambda i, j: (0, 0)),
            pl.BlockSpec((D_MODEL, 1024),
                         lambda i, j: (0, jnp.where(j < 3, j, jnp.where(j == 3, 5, j - 1)))),
            tab_spec, tab_spec, tab_spec,
        ],
        out_specs=[
            pl.BlockSpec((tm, 1024), lambda i, j: (i, jnp.minimum(j, 3))),
            pl.BlockSpec((tm, 1024), lambda i, j: (i, 0)),
            pl.BlockSpec((tm, 1024), lambda i, j: (i, 0)),
        ],
        out_shape=[
            jax.ShapeDtypeStruct((m, 4096), f32),
            jax.ShapeDtypeStruct((m, 1024), f32),
            jax.ShapeDtypeStruct((m, 1024), f32),
        ],
        scratch_shapes=[pltpu.VMEM((tm, D_MODEL), bf16)],
        compiler_params=_cparams(("arbitrary", "arbitrary")),
        name="inproj0",
    )(*operands, g, w, c, s1, s2)


def _inproj1_kernel(x_ref, g_ref, w_ref, wlr_ref, z_ref, lr_ref, xn_ref):
    j = pl.program_id(1)

    @pl.when(j == 0)
    def _():
        xn_ref[...] = _rms_rows(x_ref[...], g_ref[...]).astype(bf16)
        lr_ref[...] = jnp.dot(xn_ref[...], wlr_ref[...], preferred_element_type=f32)

    acc = jnp.dot(xn_ref[...], w_ref[...], preferred_element_type=f32)

    @pl.when(j == 0)
    def _():
        z_ref[...] = acc * (GLA_KDIM ** -0.5)

    @pl.when(j != 0)
    def _():
        z_ref[...] = acc


def _inproj1(x, g, w, wlr, tm):
    m = x.shape[0]
    return pl.pallas_call(
        _inproj1_kernel,
        grid=(m // tm, 6),
        in_specs=[
            pl.BlockSpec((tm, D_MODEL), lambda i, j: (i, 0)),
            pl.BlockSpec((1, D_MODEL), lambda i, j: (0, 0)),
            pl.BlockSpec((D_MODEL, 1024), lambda i, j: (0, j)),
            pl.BlockSpec((D_MODEL, LANES), lambda i, j: (0, 0)),
        ],
        out_specs=[
            pl.BlockSpec((tm, 1024), lambda i, j: (i, j)),
            pl.BlockSpec((tm, LANES), lambda i, j: (i, 0)),
        ],
        out_shape=[
            jax.ShapeDtypeStruct((m, 6144), f32),
            jax.ShapeDtypeStruct((m, LANES), f32),
        ],
        scratch_shapes=[pltpu.VMEM((tm, D_MODEL), bf16)],
        compiler_params=_cparams(("arbitrary", "arbitrary")),
        name="inproj1",
    )(x, g, w, wlr)


def _outproj_kernel(*refs, n_parts, tiles_per_seq):
    mix_refs = refs[:n_parts]
    if tiles_per_seq:
        w_ref, h_ref, meta_ref, g_ref, o_ref = refs[n_parts:]
    else:
        w_ref, h_ref, g_ref, o_ref = refs[n_parts:]
    kw = w_ref.shape[0] // n_parts
    tm = o_ref.shape[0]
    split = (tm // 32) * 16 if tm >= 256 else tm
    for a, b in ((0, split), (split, tm)):
        if a == b:
            continue
        y = None
        for p in range(n_parts):
            t = jnp.dot(mix_refs[p][a:b, :], w_ref[p * kw:(p + 1) * kw, :],
                        preferred_element_type=f32)
            y = t if y is None else y + t
        rn = _rms_rows(y, g_ref[...])
        if tiles_per_seq:
            first = pl.program_id(0) % tiles_per_seq == 0
            lo = max(a, N_META)
            if a == 0:
                o_ref[0:N_META, :] = (jnp.where(first, meta_ref[...], h_ref[0:N_META, :])
                                      + rn[0:N_META])
            o_ref[lo:b, :] = (jnp.where(first, h_ref[lo - N_META:b - N_META, :], h_ref[lo:b, :])
                              + rn[lo - a:])
        else:
            o_ref[a:b, :] = h_ref[a:b, :] + rn


def _outproj(mix_parts, w, h, g, tm, *, n_tiles=None, row_offset=None, meta=None, tiles_per_seq=0,
             seq_rows=None):
    n_parts = len(mix_parts)
    kw = w.shape[0] // n_parts
    if n_tiles is None:
        n_tiles = mix_parts[0].shape[0] // tm
    if row_offset is not None:
        mix_spec = pl.BlockSpec((pl.Element(tm), pl.Element(kw)), lambda i: (row_offset(i), 0))
        h_specs = [pl.BlockSpec((pl.Element(tm), pl.Element(D_MODEL)),
                                lambda i: (row_offset(i), 0))]
    else:
        mix_spec = pl.BlockSpec((tm, kw), lambda i: (i, 0))
        h_specs = [pl.BlockSpec((tm, D_MODEL), lambda i: (i, 0))]
    operands = [h]
    if meta is not None:
        h_specs = [
            pl.BlockSpec((pl.Element(tm), pl.Element(D_MODEL)),
                         lambda i: (_seq_tile_offset(i, tm, tiles_per_seq, seq_rows), 0)),
            pl.BlockSpec((N_META, D_MODEL), lambda i: (0, 0)),
        ]
        operands = [h, meta]
    return pl.pallas_call(
        functools.partial(_outproj_kernel, n_parts=n_parts, tiles_per_seq=tiles_per_seq),
        grid=(n_tiles,),
        in_specs=[mix_spec] * n_parts
                 + [pl.BlockSpec(w.shape, lambda i: (0, 0), pipeline_mode=pl.Buffered(1))]
                 + h_specs + [pl.BlockSpec((1, D_MODEL), lambda i: (0, 0))],
        out_specs=pl.BlockSpec((tm, D_MODEL), lambda i: (i, 0)),
        out_shape=jax.ShapeDtypeStruct((n_tiles * tm, D_MODEL), f32),
        compiler_params=_cparams(("arbitrary",)),
        name="outproj",
    )(*mix_parts, w, *operands, g)


def _pool_finish(window_sum, u, cnt, pw, scale, gate):
    d = window_sum / cnt - u
    po = jnp.dot(d.astype(bf16), pw, preferred_element_type=f32) * scale
    return (po * (gate * _sigmoid(gate))).astype(bf16)


def _pool_prompt_kernel(u_ref, halo_ref, pg_ref, pw_ref, sc_ref, o_ref, ext_ref, *, tiles_per_seq):
    tm = u_ref.shape[0]
    r = pl.program_id(0) % tiles_per_seq
    halo = halo_ref[...]
    ext_ref[0:16, :] = jnp.where(r == 0, jnp.zeros_like(halo), halo)
    ext_ref[16:, :] = u_ref[...]
    pos = r * tm + lax.broadcasted_iota(jnp.int32, (tm, 1), 0)
    for gi, w in enumerate(POOL_WINDOWS):
        cs = slice(gi * POOL_GROUP, (gi + 1) * POOL_GROUP)
        u = ext_ref[16:16 + tm, cs]
        s = u
        for t in range(1, w):
            s = s + ext_ref[16 - t:16 - t + tm, cs]
        cnt = jnp.minimum(w, pos + 1).astype(f32)
        o_ref[:, cs] = _pool_finish(s, u, cnt, pw_ref[gi], sc_ref[:, cs], pg_ref[:, cs])


def _pool_prompt(z4, pool_w, pool_scale, tm, tiles_per_seq):
    m = z4.shape[0]
    hb = tm // 16
    return pl.pallas_call(
        functools.partial(_pool_prompt_kernel, tiles_per_seq=tiles_per_seq),
        grid=(m // tm,),
        in_specs=[
            pl.BlockSpec((tm, 1024), lambda i: (i, 0)),
            pl.BlockSpec((16, 1024), lambda i: (jnp.maximum(i * hb - 1, 0), 0)),
            pl.BlockSpec((tm, 1024), lambda i: (i, 1)),
            pl.BlockSpec((4, POOL_GROUP, POOL_GROUP), lambda i: (0, 0, 0)),
            pl.BlockSpec((1, 1024), lambda i: (0, 0)),
        ],
        out_specs=pl.BlockSpec((tm, 1024), lambda i: (i, 0)),
        out_shape=jax.ShapeDtypeStruct((m, 1024), bf16),
        scratch_shapes=[pltpu.VMEM((tm + 16, 1024), f32)],
        compiler_params=_cparams(("arbitrary",)),
        name="pool_prompt",
    )(z4, z4, z4, pool_w, pool_scale)


def _pool_sample_kernel(u_ref, prev_ref, pg_ref, pw_ref, sc_ref, o_ref):
    for gi, w in enumerate(POOL_WINDOWS):
        cs = slice(gi * POOL_GROUP, (gi + 1) * POOL_GROUP)
        u = u_ref[:, cs]
        s = u
        for t in range(1, w):
            base = (POOL_CTX - t) * 1024 + gi * POOL_GROUP
            s = s + prev_ref[:, base:base + POOL_GROUP]
        o_ref[:, cs] = _pool_finish(s, u, float(w), pw_ref[gi], sc_ref[:, cs], pg_ref[:, cs])


def _pool_sample(z4, prev_flat, pool_w, pool_scale):
    m = z4.shape[0]
    return pl.pallas_call(
        _pool_sample_kernel,
        grid=(1,),
        in_specs=[
            pl.BlockSpec((m, 1024), lambda i: (0, 0)),
            pl.BlockSpec(prev_flat.shape, lambda i: (0, 0)),
            pl.BlockSpec((m, 1024), lambda i: (0, 1)),
            pl.BlockSpec((4, POOL_GROUP, POOL_GROUP), lambda i: (0, 0, 0)),
            pl.BlockSpec((1, 1024), lambda i: (0, 0)),
        ],
        out_specs=pl.BlockSpec((m, 1024), lambda i: (0, 0)),
        out_shape=jax.ShapeDtypeStruct((m, 1024), bf16),
        compiler_params=_cparams(("arbitrary",)),
        name="pool_sample",
    )(z4, prev_flat, z4, pool_w, pool_scale)


def _lambda(lq1, lk1, lq2, lk2):
    a = jnp.sum(lq1[...] * lk1[...], axis=-1, keepdims=True)
    b = jnp.sum(lq2[...] * lk2[...], axis=-1, keepdims=True)
    return jnp.exp(a) - jnp.exp(b) + LAMBDA_INIT


def _attn_finish(o, sub, gate):
    on = _rms_rows(o, sub) * (1.0 - LAMBDA_INIT)
    return (on * (gate * _sigmoid(gate))).astype(bf16)


def _attn_prompt_kernel(q_ref, k_ref, v_ref, ag_ref, lq1, lk1, lq2, lk2, sub_ref, o_ref,
                        kb_ref, vb_ref):
    kb_ref[...] = k_ref[...].astype(bf16)
    vb_ref[...] = v_ref[...].astype(bf16)
    lam = _lambda(lq1, lk1, lq2, lk2)
    first_map = lax.broadcasted_iota(jnp.int32, (1, LANES), 1) < DA_QKDIM
    n_tiles = (q_ref.shape[0] - N_META) // ATTN_TQ

    def stack_maps(q):
        q = q * LOG2E
        return jnp.concatenate([jnp.where(first_map, q, 0.0), jnp.where(first_map, 0.0, q)],
                               axis=0).astype(bf16)

    def causal(t):
        row = lax.broadcasted_iota(jnp.int32, (2 * t, t), 0)
        col = lax.broadcasted_iota(jnp.int32, (2 * t, t), 1)
        return jnp.where(row >= t, row - t, row) >= col

    def step(qq, carry, kt, vt, mask):
        m, l, acc = carry
        s = lax.dot_general(qq, kt, NT_DIMS, preferred_element_type=f32)
        if mask is not None:
            s = jnp.where(mask, s, -jnp.inf)
        m_new = jnp.maximum(m, jnp.max(s, axis=-1, keepdims=True))
        alpha = jnp.exp2(m - m_new)
        p = jnp.exp2(s - m_new)
        l = alpha * l + jnp.sum(p, axis=-1, keepdims=True)
        acc = alpha * acc + jnp.dot(p.astype(bf16), vt, preferred_element_type=f32)
        return m_new, l, acc

    def init(t):
        return (jnp.full((2 * t, 1), -jnp.inf, f32), jnp.zeros((2 * t, 1), f32),
                jnp.zeros((2 * t, LANES), f32))

    def finish(carry, t, rows):
        _, l, acc = carry
        o = acc[:t] / l[:t] - lam * (acc[t:] / l[t:])
        o_ref[rows, :] = _attn_finish(o, sub_ref[...], ag_ref[rows, :])

    meta = slice(0, N_META)
    qq = stack_maps(q_ref[meta, :])
    carry = step(qq, init(N_META), kb_ref[meta, :], vb_ref[meta, :], causal(N_META))
    finish(carry, N_META, meta)

    diag_mask = causal(ATTN_TQ)

    for i in range(n_tiles):
        rows = slice(N_META + i * ATTN_TQ, N_META + (i + 1) * ATTN_TQ)
        qq = stack_maps(q_ref[rows, :])
        carry = step(qq, init(ATTN_TQ), kb_ref[meta, :], vb_ref[meta, :], None)
        for j in range(i):
            cols = slice(N_META + j * ATTN_TQ, N_META + (j + 1) * ATTN_TQ)
            carry = step(qq, carry, kb_ref[cols, :], vb_ref[cols, :], None)
        carry = step(qq, carry, kb_ref[rows, :], vb_ref[rows, :], diag_mask)
        finish(carry, ATTN_TQ, rows)


def _attn_prompt(z4, k, v, lams, subln, n_seq, seq_len):
    m = z4.shape[0]
    vec = pl.BlockSpec((1, DA_QKDIM), lambda b, h: (0, 0))
    return pl.pallas_call(
        _attn_prompt_kernel,
        grid=(n_seq, DA_HEADS),
        in_specs=[
            pl.BlockSpec((seq_len, LANES), lambda b, h: (b, 16 + h)),
            pl.BlockSpec((seq_len, LANES), lambda b, h: (b, h)),
            pl.BlockSpec((seq_len, LANES), lambda b, h: (b, h)),
            pl.BlockSpec((seq_len, LANES), lambda b, h: (b, 24 + h)),
            vec, vec, vec, vec,
            pl.BlockSpec((1, LANES), lambda b, h: (0, 0)),
        ],
        out_specs=pl.BlockSpec((seq_len, LANES), lambda b, h: (b, h)),
        out_shape=jax.ShapeDtypeStruct((m, 1024), bf16),
        scratch_shapes=[pltpu.VMEM((seq_len, LANES), bf16), pltpu.VMEM((seq_len, LANES), bf16)],
        compiler_params=_cparams(("arbitrary", "arbitrary")),
        name="attn_prompt",
    )(z4, k, v, z4, *lams, subln)


def _half_sum_matrix():
    d = np.arange(LANES)[:, None] < DA_QKDIM
    c = np.arange(LANES)[None, :] < DA_QKDIM
    return (d == c).astype(np.float32)


def _attn_sample_kernel(pt_ref, q_ref, kn_ref, vn_ref, ag_ref, lq1, lk1, lq2, lk2, sub_ref, e_ref,
                        *rest, pages):
    del pt_ref
    kp_refs, vp_refs = rest[:pages], rest[pages:2 * pages]
    o_ref, m_ref, l_ref, acc_ref, accr_ref = rest[2 * pages:]
    p = pl.program_id(1)
    q = q_ref[0] * LOG2E
    first_half = lax.broadcasted_iota(jnp.int32, (DA_HEADS, LANES), 1) < DA_QKDIM
    swap = lambda a: pltpu.roll(a, DA_QKDIM, a.ndim - 1)

    @pl.when(p == 0)
    def _():
        prod = kn_ref[0] * q
        s0 = jnp.sum(jnp.where(first_half, prod, 0.0), axis=-1, keepdims=True)
        s1 = jnp.sum(jnp.where(first_half, 0.0, prod), axis=-1, keepdims=True)
        m_ref[...] = jnp.where(first_half, s0, s1)
        l_ref[...] = jnp.ones_like(l_ref)
        acc_ref[...] = vn_ref[0]
        accr_ref[...] = swap(vn_ref[0])

    m, l, acc, acc_r = m_ref[...], l_ref[...], acc_ref[...], accr_ref[...]
    for pg in range(pages):
        for st in range(PAGE // ATTN_SUB):
            rows = slice(st * ATTN_SUB, (st + 1) * ATTN_SUB)
            kp = kp_refs[pg][0, rows]
            vp = vp_refs[pg][0, rows]
            prod = (kp * q[None]).reshape(ATTN_SUB * DA_HEADS, LANES).astype(bf16)
            r = jnp.dot(prod, e_ref[...], preferred_element_type=f32)
            r = r.reshape(ATTN_SUB, DA_HEADS, LANES)
            m_new = jnp.maximum(m, jnp.max(r, axis=0))
            alpha = jnp.exp2(m - m_new)
            pw = jnp.exp2(r - m_new[None])
            l = alpha * l + jnp.sum(pw, axis=0)
            acc = alpha * acc + jnp.sum(pw * vp, axis=0)
            acc_r = alpha * acc_r + jnp.sum(pw * swap(vp), axis=0)
            m = m_new
    m_ref[...], l_ref[...], acc_ref[...], accr_ref[...] = m, l, acc, acc_r

    @pl.when(p == pl.num_programs(1) - 1)
    def _():
        lam = _lambda(lq1, lk1, lq2, lk2)
        other = swap(acc_r)
        l_sw = swap(l)
        o0 = jnp.where(first_half, acc, other) / jnp.where(first_half, l, l_sw)
        o1 = jnp.where(first_half, other, acc) / jnp.where(first_half, l_sw, l)
        o_ref[0] = _attn_finish(o0 - lam * o1, sub_ref[...], ag_ref[0])


def _attn_sample(page_table, q, k_new, v_new, ag, lams, subln, cache_k, cache_v, pages=8):
    bd, n_pages = page_table.shape
    assert n_pages % pages == 0
    tok = pl.BlockSpec((1, DA_HEADS, LANES), lambda b, p, pt: (b, 0, 0))
    vec = pl.BlockSpec((1, DA_QKDIM), lambda b, p, pt: (0, 0))

    def page(i):
        return pl.BlockSpec((1, PAGE, DA_HEADS, LANES),
                            lambda b, p, pt: (pt[b * n_pages + p * pages + i], 0, 0, 0))

    e = jnp.asarray(_half_sum_matrix(), dtype=bf16)
    grid_spec = pltpu.PrefetchScalarGridSpec(
        num_scalar_prefetch=1,
        grid=(bd, n_pages // pages),
        in_specs=[tok, tok, tok, tok, vec, vec, vec, vec,
                  pl.BlockSpec((1, LANES), lambda b, p, pt: (0, 0)),
                  pl.BlockSpec((LANES, LANES), lambda b, p, pt: (0, 0))]
                 + [page(i) for i in range(pages)] * 2,
        out_specs=tok,
        scratch_shapes=[pltpu.VMEM((DA_HEADS, LANES), f32) for _ in range(4)],
    )
    return pl.pallas_call(
        functools.partial(_attn_sample_kernel, pages=pages),
        grid_spec=grid_spec,
        out_shape=jax.ShapeDtypeStruct((bd, DA_HEADS, LANES), bf16),
        compiler_params=_cparams(("arbitrary", "arbitrary")),
        name="attn_sample",
    )(page_table.reshape(-1), q, k_new, v_new, ag, *lams, subln, e,
      *([cache_k] * pages), *([cache_v] * pages))


def _gla_consts(c):
    n = int(round(math.log2(c)))
    t = np.arange(c)[:, None]
    u = np.arange(c)[None, :]
    sums = [(u <= t), (u > t)]
    masks = [(u == t)]
    for lvl in range(n):
        half = c >> (lvl + 1)
        par = 2 * half
        split = (t // par) * par + half - 1
        upper = (t % par) >= half
        sums.append((upper & (u > split) & (u <= t)) | ((~upper) & (u > t) & (u <= split)))
        masks.append(upper & ((u % par) < half) & ((u // par) == (t // par)))
    sums = np.stack(sums).astype(np.float32).reshape(-1, c)
    return np.concatenate([sums] * 3, axis=1), np.stack(masks).astype(np.float32)


def _split3(x):
    hi = x.astype(bf16)
    r = x - hi.astype(f32)
    mid = r.astype(bf16)
    lo = (r - mid.astype(f32)).astype(bf16)
    return jnp.concatenate([hi, mid, lo], axis=0)


def _log_sigmoid(x):
    return jnp.minimum(x, 0.0) - jnp.log1p(jnp.exp(-jnp.abs(x)))


def _log_decay(lr, wa, ba):
    x = jnp.dot(lr.astype(bf16), wa, preferred_element_type=f32) + ba
    return _log_sigmoid(x) * (1.0 / GLA_TAU)


def _gla_finish(o, gn, gate):
    return (_rms_rows(o, gn) * (gate * _sigmoid(gate))).astype(bf16)


def _gla_prompt_kernel(q_ref, k_ref, v_ref, gate_ref, lr_ref, wa_ref, ba_ref, gn_ref,
                       sum16_ref, msk16_ref, sum64_ref, msk64_ref, o_ref, st_ref, g_ref, s_ref):
    g_ref[...] = _log_decay(lr_ref[...], wa_ref[...], ba_ref[...]) * LOG2E

    def chunk(st, r0, c, sum_ref, msk_ref):
        n = msk_ref.shape[0] - 1
        rows = pl.ds(r0, c)
        q = q_ref[rows, :]
        k = k_ref[rows, :]
        vb = v_ref[rows, :].astype(bf16)
        e = jnp.exp2(jnp.dot(sum_ref[...], _split3(g_ref[rows, :]),
                             preferred_element_type=f32))
        o = lax.dot_general((q * e[0:c]).astype(bf16), st.astype(bf16), NT_DIMS,
                            preferred_element_type=f32)
        sc = msk_ref[0] * lax.dot_general(q.astype(bf16), k.astype(bf16), NT_DIMS,
                                          preferred_element_type=f32)
        for lvl in range(n):
            el = e[(2 + lvl) * c:(3 + lvl) * c]
            sc = sc + msk_ref[1 + lvl] * lax.dot_general(
                (q * el).astype(bf16), (k * el).astype(bf16), NT_DIMS, preferred_element_type=f32)
        o = o + jnp.dot(sc.astype(bf16), vb, preferred_element_type=f32)
        k_end = (k * e[c:2 * c]).astype(bf16)
        o_ref[rows, :] = _gla_finish(o, gn_ref[...], gate_ref[rows, :])
        return st * e[c - 1:c] + lax.dot_general(vb, k_end, TN_DIMS, preferred_element_type=f32)

    s_ref[...] = chunk(jnp.zeros(s_ref.shape, f32), 0, N_META, sum16_ref, msk16_ref)

    def body(ci, _):
        st = s_ref[...]
        for u in range(GLA_UNROLL):
            r0 = pl.multiple_of(N_META + (ci * GLA_UNROLL + u) * GLA_CHUNK, 16)
            st = chunk(st, r0, GLA_CHUNK, sum64_ref, msk64_ref)
        s_ref[...] = st
        return 0

    n_chunks = (q_ref.shape[0] - N_META) // GLA_CHUNK
    assert n_chunks % GLA_UNROLL == 0
    lax.fori_loop(0, n_chunks // GLA_UNROLL, body, 0)
    st_ref[0, 0] = s_ref[...].T


def _gla_prompt(z1, lr, wa, ba, gn, n_seq, seq_len):
    m = z1.shape[0]
    sum16, msk16 = _gla_consts(N_META)
    sum64, msk64 = _gla_consts(GLA_CHUNK)
    consts = [jnp.asarray(sum16, bf16), jnp.asarray(msk16), jnp.asarray(sum64, bf16),
              jnp.asarray(msk64)]

    def full(a):
        nd = a.ndim
        return pl.BlockSpec(a.shape, lambda b, h: (0,) * nd)

    return pl.pallas_call(
        _gla_prompt_kernel,
        grid=(n_seq, GLA_HEADS),
        in_specs=[
            pl.BlockSpec((seq_len, GLA_KDIM), lambda b, h: (b, h)),
            pl.BlockSpec((seq_len, GLA_KDIM), lambda b, h: (b, 4 + h)),
            pl.BlockSpec((seq_len, GLA_VDIM), lambda b, h: (b, 4 + h)),
            pl.BlockSpec((seq_len, GLA_VDIM), lambda b, h: (b, 8 + h)),
            pl.BlockSpec((seq_len, LANES), lambda b, h: (b, 0)),
            pl.BlockSpec((LANES, GLA_KDIM), lambda b, h: (0, h)),
            pl.BlockSpec((1, GLA_KDIM), lambda b, h: (0, h)),
            pl.BlockSpec((1, GLA_VDIM), lambda b, h: (0, 0)),
        ] + [full(a) for a in consts],
        out_specs=[
            pl.BlockSpec((seq_len, GLA_VDIM), lambda b, h: (b, h)),
            pl.BlockSpec((1, 1, GLA_KDIM, GLA_VDIM), lambda b, h: (b, h, 0, 0)),
        ],
        out_shape=[
            jax.ShapeDtypeStruct((m, GLA_HEADS * GLA_VDIM), bf16),
            jax.ShapeDtypeStruct((n_seq, GLA_HEADS, GLA_KDIM, GLA_VDIM), f32),
        ],
        scratch_shapes=[pltpu.VMEM((seq_len, GLA_KDIM), f32), pltpu.VMEM((GLA_VDIM, GLA_KDIM), f32)],
        compiler_params=_cparams(("arbitrary", "arbitrary")),
        name="gla_prompt",
    )(z1, z1, z1, z1, lr, wa, ba, gn, *consts)


def _gla_sample_kernel(z_ref, lr_ref, wa_ref, wat_ref, ba_ref, bat_ref, gn_ref, s_ref,
                       o_ref, st_ref):
    lr = lr_ref[0].astype(bf16)
    lr_rows = jnp.broadcast_to(lr, (LANES, LANES))
    first_row = lax.broadcasted_iota(jnp.int32, (16, 1), 0) == 0
    kw, vw = GLA_HEADS * GLA_KDIM, GLA_HEADS * GLA_VDIM
    for h in range(GLA_HEADS):
        kc = slice(h * GLA_KDIM, (h + 1) * GLA_KDIM)
        vc = slice(h * GLA_VDIM, (h + 1) * GLA_VDIM)
        q = z_ref[0, :, kc]
        k = z_ref[0, :, kw + h * GLA_KDIM:kw + (h + 1) * GLA_KDIM]
        v = z_ref[0, :, 2 * kw + h * GLA_VDIM:2 * kw + (h + 1) * GLA_VDIM]
        gate = z_ref[0, :, 2 * kw + vw + h * GLA_VDIM:2 * kw + vw + (h + 1) * GLA_VDIM]
        x_row = jnp.dot(lr, wa_ref[:, kc], preferred_element_type=f32) + ba_ref[:, kc]
        x_col = lax.dot_general(wat_ref[kc, :], lr_rows, NT_DIMS,
                                preferred_element_type=f32) + bat_ref[kc, :]
        eg_row = jnp.exp(_log_sigmoid(x_row) * (1.0 / GLA_TAU))
        eg_col = jnp.exp(_log_sigmoid(x_col) * (1.0 / GLA_TAU))
        s = s_ref[0, h]
        o = jnp.dot((q * eg_row).astype(bf16), s.astype(bf16), preferred_element_type=f32)
        o = o + jnp.sum(q * k, axis=-1, keepdims=True) * v
        k16 = jnp.where(first_row, k, 0.0).astype(bf16)
        v16 = jnp.broadcast_to(v, (16, GLA_VDIM)).astype(bf16)
        st_ref[0, h] = (s * jnp.concatenate([eg_col] * (GLA_VDIM // LANES), axis=1)
                        + lax.dot_general(k16, v16, TN_DIMS, preferred_element_type=f32))
        o_ref[0, :, vc] = _gla_finish(o, gn_ref[...], gate)


def _gla_sample(z1, lr, wa, ba, gn, state):
    bd = state.shape[0]
    kw = GLA_HEADS * GLA_KDIM
    wat = wa.T
    bat = jnp.broadcast_to(ba.reshape(kw, 1), (kw, LANES))
    const = lambda a: pl.BlockSpec(a.shape, lambda b: (0,) * a.ndim)
    st_spec = pl.BlockSpec((1, GLA_HEADS, GLA_KDIM, GLA_VDIM), lambda b: (b, 0, 0, 0))
    return pl.pallas_call(
        _gla_sample_kernel,
        grid=(bd,),
        in_specs=[
            pl.BlockSpec((1, 1, z1.shape[2]), lambda b: (b, 0, 0)),
            pl.BlockSpec((1, 1, LANES), lambda b: (b, 0, 0)),
            const(wa), const(wat), const(ba), const(bat), const(gn), st_spec,
        ],
        out_specs=[
            pl.BlockSpec((1, 1, GLA_HEADS * GLA_VDIM), lambda b: (b, 0, 0)),
            st_spec,
        ],
        out_shape=[
            jax.ShapeDtypeStruct((bd, 1, GLA_HEADS * GLA_VDIM), bf16),
            jax.ShapeDtypeStruct(state.shape, f32),
        ],
        compiler_params=_cparams(("arbitrary",)),
        name="gla_sample",
    )(z1, lr, wa, wat, ba, bat, gn, state)


def _rope_tables(positions):
    half = ROT_DIM // 2
    inv = jnp.power(ROPE_THETA, -jnp.arange(half, dtype=f32) * 2.0 / ROT_DIM)
    ang = positions.astype(f32)[:, None] * inv[None, :]
    cos, sin = jnp.cos(ang), jnp.sin(ang)
    t = positions.shape[0]
    rest = DA_QKDIM - ROT_DIM
    z8 = jnp.zeros((t, half), f32)
    zr = jnp.zeros((t, rest), f32)
    c = jnp.concatenate([cos, cos, jnp.ones((t, rest), f32)], axis=1)
    s1 = jnp.concatenate([z8, sin, zr], axis=1)
    s2 = jnp.concatenate([-sin, z8, zr], axis=1)
    return tuple(jnp.concatenate([a, a], axis=1) for a in (c, s1, s2))


def kernel(x_prompt, x_sample, cache_k, cache_v, state_pool, state_gla, page_table, meta_tokens,
           pre_norm_0, post_norm_0, w_in_0, pool_w_0, pool_scale_0, lambda_q1_0, lambda_k1_0,
           lambda_q2_0, lambda_k2_0, subln_0, w_out_0,
           pre_norm_1, post_norm_1, w_in_1, gla_wa2_1, gla_ba_1, gla_norm_1, w_out_1):
    B, seq = x_prompt.shape[0], x_prompt.shape[1]
    Bd = x_sample.shape[0]
    L = seq + N_META
    assert L % ROW_TILE == 0 and (L - N_META) % ATTN_TQ == 0 and x_sample.shape[1] == 1

    row = lambda a: a.reshape(1, -1)
    w0 = w_in_0.astype(bf16)
    w1 = w_in_1[:, :6144].astype(bf16)
    w1_lr = jnp.pad(w_in_1[:, 6144:], ((0, 0), (0, LANES - GLA_RANK))).astype(bf16)
    wa = jnp.pad(gla_wa2_1, ((0, LANES - GLA_RANK), (0, 0))).astype(bf16)
    wo0 = w_out_0.astype(bf16)
    wo1 = w_out_1.astype(bf16)
    pw = pool_w_0.astype(bf16)
    lams = [row(a) for a in (lambda_q1_0, lambda_k1_0, lambda_q2_0, lambda_k2_0)]

    meta = meta_tokens.astype(x_prompt.dtype)
    x_p = x_prompt.reshape(B * seq, D_MODEL)
    h_s = x_sample.reshape(Bd, D_MODEL)
    tiles = L // ROW_TILE
    tabs_p = _rope_tables(jnp.arange(L, dtype=jnp.int32))
    past_len = page_table.shape[1] * PAGE
    tabs_s = _rope_tables(jnp.full((Bd,), past_len, dtype=jnp.int32))

    z4_p, k_p, v_p = _inproj0(x_p, row(pre_norm_0), w0, tabs_p, ROW_TILE, tiles, meta=meta, n_seq=B)
    mix_a = _pool_prompt(z4_p, pw, row(pool_scale_0), ROW_TILE, tiles)
    mix_b = _attn_prompt(z4_p, k_p, v_p, lams, row(subln_0), B, L)
    h_p = _outproj([mix_a, mix_b], wo0, x_p, row(post_norm_0), ROW_TILE, meta=meta,
                   tiles_per_seq=tiles, seq_rows=seq)

    z4_s, k_s, v_s = _inproj0(h_s, row(pre_norm_0), w0, tabs_s, Bd, 1)
    mix_a_s = _pool_sample(z4_s, state_pool.reshape(Bd, POOL_CTX * 1024), pw, row(pool_scale_0))
    hd = lambda a: a.reshape(Bd, DA_HEADS, LANES)
    mix_b_s = _attn_sample(page_table, hd(z4_s[:, 2048:3072]), hd(k_s), hd(v_s),
                           hd(z4_s[:, 3072:4096]), lams, row(subln_0), cache_k, cache_v)
    h_s = _outproj([mix_a_s, mix_b_s.reshape(Bd, 1024)], wo0, h_s, row(post_norm_0), Bd)

    z1_p, lr_p = _inproj1(h_p, row(pre_norm_1), w1, w1_lr, ROW_TILE)
    mix1_p, gla_p = _gla_prompt(z1_p, lr_p, wa, row(gla_ba_1), row(gla_norm_1), B, L)
    per_seq = seq // OUT_TILE
    y_p = _outproj([mix1_p], wo1, h_p, row(post_norm_1), OUT_TILE, n_tiles=B * per_seq,
                   row_offset=lambda i: pl.multiple_of(
                       (i // per_seq) * L + N_META + (i % per_seq) * OUT_TILE, 16))

    z1_s, lr_s = _inproj1(h_s, row(pre_norm_1), w1, w1_lr, Bd)
    mix1_s, gla_s = _gla_sample(z1_s.reshape(Bd, 1, 6144), lr_s.reshape(Bd, 1, LANES), wa,
                                row(gla_ba_1), row(gla_norm_1), state_gla)
    h_s = _outproj([mix1_s.reshape(Bd, 2048)], wo1, h_s, row(post_norm_1), Bd)

    pool_p = lax.slice(z4_p.reshape(B, L, 4096), (0, L - POOL_CTX, 0), (B, L, 1024))
    pool_s = jnp.concatenate([state_pool[:, 1:], z4_s[:, None, :1024]], axis=1)
    return (y_p.reshape(B, seq, D_MODEL), h_s.reshape(Bd, 1, D_MODEL),
            k_p.reshape(B, L, DA_HEADS, LANES), v_p.reshape(B, L, DA_HEADS, LANES), pool_p, gla_p,
            k_s.reshape(Bd, 1, DA_HEADS, LANES), v_s.reshape(Bd, 1, DA_HEADS, LANES), pool_s, gla_s)
```

```python
import functools
import math

import numpy as np
import jax
import jax.numpy as jnp
from jax import lax
from jax.experimental import pallas as pl
from jax.experimental.pallas import tpu as pltpu

f32 = jnp.float32
bf16 = jnp.bfloat16

D_MODEL = 2048
N_META = 16
NORM_EPS = 1e-6
POOL_WINDOWS = (2, 4, 8, 16)
POOL_GROUP = 256
POOL_CTX = 15
DA_HEADS = 8
DA_VDIM = 128
DA_QKDIM = 64
DA_SCALE = DA_QKDIM ** -0.5
ROT_DIM = 16
ROPE_THETA = 500000.0
LAMBDA_INIT = 0.8 - 0.6 * math.exp(-0.3 * 0)
GLA_HEADS = 4
GLA_KDIM = 256
GLA_VDIM = 512
GLA_RANK = 16
GLA_TAU = 16.0
GLA_CHUNK = 64
GLA_UNROLL = 2
PAGE = 128

LANES = 128
ROW_TILE = 688
OUT_TILE = 512
FUSED_TN = 256
ATTN_TQ = 512
ATTN_SUB = 8
LOG2E = 1.4426950408889634
VMEM_LIMIT = 52 * 1024 * 1024

NT_DIMS = (((1,), (1,)), ((), ()))
TN_DIMS = (((0,), (0,)), ((), ()))


def _cparams(sem):
    return pltpu.CompilerParams(dimension_semantics=sem, vmem_limit_bytes=VMEM_LIMIT)


def _sigmoid(x):
    return 1.0 / (1.0 + jnp.exp(-x))


def _rms_rows(x, g):
    ms = jnp.mean(x * x, axis=-1, keepdims=True)
    return x * lax.rsqrt(ms + NORM_EPS) * g


def _rope_store(dst_ref, a, c, s1, s2, scale):
    for h in range(DA_HEADS):
        xh = a[:, h * LANES:(h + 1) * LANES]
        r = xh * c + pltpu.roll(xh, 8, 1) * s1 + pltpu.roll(xh, LANES - 8, 1) * s2
        dst_ref[:, h * LANES:(h + 1) * LANES] = r * scale if scale != 1.0 else r


def _seq_tile_offset(i, tm, tiles_per_seq, seq_rows):
    r = i % tiles_per_seq
    return pl.multiple_of((i // tiles_per_seq) * seq_rows + jnp.maximum(r * tm - N_META, 0), 16)


def _inproj0_kernel(*refs, tiles_per_seq):
    if tiles_per_seq:
        x_ref, meta_ref, g_ref, w_ref, c_ref, s1_ref, s2_ref, z_ref, k_ref, v_ref, xn_ref = refs
    else:
        x_ref, g_ref, w_ref, c_ref, s1_ref, s2_ref, z_ref, k_ref, v_ref, xn_ref = refs
    j = pl.program_id(1)
    tm = xn_ref.shape[0]

    @pl.when(j == 0)
    def _():
        if tiles_per_seq:
            first = pl.program_id(0) % tiles_per_seq == 0

            @pl.when(first)
            def _():
                xn_ref[0:N_META, :] = _rms_rows(meta_ref[...], g_ref[...]).astype(bf16)
                xn_ref[N_META:, :] = _rms_rows(x_ref[0:tm - N_META, :], g_ref[...]).astype(bf16)

            @pl.when(jnp.logical_not(first))
            def _():
                xn_ref[...] = _rms_rows(x_ref[...], g_ref[...]).astype(bf16)
        else:
            xn_ref[...] = _rms_rows(x_ref[...], g_ref[...]).astype(bf16)

    acc = jnp.dot(xn_ref[...], w_ref[...], preferred_element_type=f32)

    @pl.when((j == 0) | (j == 1) | (j == 3))
    def _():
        z_ref[...] = acc

    @pl.when(j == 2)
    def _():
        _rope_store(z_ref, acc, c_ref[...], s1_ref[...], s2_ref[...], DA_SCALE)

    @pl.when(j == 4)
    def _():
        _rope_store(k_ref, acc, c_ref[...], s1_ref[...], s2_ref[...], 1.0)

    @pl.when(j == 5)
    def _():
        v_ref[...] = acc


def _inproj0(x, g, w, tabs, tm, tab_blocks, meta=None, n_seq=None):
    c, s1, s2 = tabs
    tab_spec = pl.BlockSpec((tm, LANES), lambda i, j: (i % tab_blocks, 0))
    if meta is None:
        m = x.shape[0]
        x_specs = [pl.BlockSpec((tm, D_MODEL), lambda i, j: (i, 0))]
        operands = (x,)
    else:
        seq_rows = x.shape[0] // n_seq
        m = x.shape[0] + n_seq * N_META
        x_specs = [
            pl.BlockSpec((pl.Element(tm), pl.Element(D_MODEL)),
                         lambda i, j: (_seq_tile_offset(i, tm, tab_blocks, seq_rows), 0)),
            pl.BlockSpec((N_META, D_MODEL), lambda i, j: (0, 0)),
        ]
        operands = (x, meta)
    return pl.pallas_call(
        functools.partial(_inproj0_kernel, tiles_per_seq=tab_blocks if meta is not None else 0),
        grid=(m // tm, 6),
        in_specs=x_specs + [
            pl.BlockSpec((1, D_MODEL), lambda i, j: (0, 0)),
            pl.BlockSpec((D_MODEL, 1024),
                         lambda i, j: (0, jnp.where(j < 3, j, jnp.where(j == 3, 5, j - 1)))),
            tab_spec, tab_spec, tab_spec,
        ],
        out_specs=[
            pl.BlockSpec((tm, 1024), lambda i, j: (i, jnp.minimum(j, 3))),
            pl.BlockSpec((tm, 1024), lambda i, j: (i, 0)),
            pl.BlockSpec((tm, 1024), lambda i, j: (i, 0)),
        ],
        out_shape=[
            jax.ShapeDtypeStruct((m, 4096), f32),
            jax.ShapeDtypeStruct((m, 1024), f32),
            jax.ShapeDtypeStruct((m, 1024), f32),
        ],
        scratch_shapes=[pltpu.VMEM((tm, D_MODEL), bf16)],
        compiler_params=_cparams(("arbitrary", "arbitrary")),
        name="inproj0",
    )(*operands, g, w, c, s1, s2)


def _inproj1_kernel(x_ref, g_ref, w_ref, wlr_ref, z_ref, lr_ref, xn_ref):
    j = pl.program_id(1)

    @pl.when(j == 0)
    def _():
        xn_ref[...] = _rms_rows(x_ref[...], g_ref[...]).astype(bf16)
        lr_ref[...] = jnp.dot(xn_ref[...], wlr_ref[...], preferred_element_type=f32)

    acc = jnp.dot(xn_ref[...], w_ref[...], preferred_element_type=f32)

    @pl.when(j == 0)
    def _():
        z_ref[...] = acc * (GLA_KDIM ** -0.5)

    @pl.when(j != 0)
    def _():
        z_ref[...] = acc


def _inproj1(x, g, w, wlr, tm):
    m = x.shape[0]
    return pl.pallas_call(
        _inproj1_kernel,
        grid=(m // tm, 6),
        in_specs=[
            pl.BlockSpec((tm, D_MODEL), lambda i, j: (i, 0)),
            pl.BlockSpec((1, D_MODEL), lambda i, j: (0, 0)),
            pl.BlockSpec((D_MODEL, 1024), lambda i, j: (0, j)),
            pl.BlockSpec((D_MODEL, LANES), lambda i, j: (0, 0)),
        ],
        out_specs=[
            pl.BlockSpec((tm, 1024), lambda i, j: (i, j)),
            pl.BlockSpec((tm, LANES), lambda i, j: (i, 0)),
        ],
        out_shape=[
            jax.ShapeDtypeStruct((m, 6144), f32),
            jax.ShapeDtypeStruct((m, LANES), f32),
        ],
        scratch_shapes=[pltpu.VMEM((tm, D_MODEL), bf16)],
        compiler_params=_cparams(("arbitrary", "arbitrary")),
        name="inproj1",
    )(x, g, w, wlr)


def _outproj_kernel(*refs, n_parts, tiles_per_seq):
    mix_refs = refs[:n_parts]
    if tiles_per_seq:
        w_ref, h_ref, meta_ref, g_ref, o_ref = refs[n_parts:]
    else:
        w_ref, h_ref, g_ref, o_ref = refs[n_parts:]
    kw = w_ref.shape[0] // n_parts
    tm = o_ref.shape[0]
    split = (tm // 32) * 16 if tm >= 256 else tm
    for a, b in ((0, split), (split, tm)):
        if a == b:
            continue
        y = None
        for p in range(n_parts):
            t = jnp.dot(mix_refs[p][a:b, :], w_ref[p * kw:(p + 1) * kw, :],
                        preferred_element_type=f32)
            y = t if y is None else y + t
        rn = _rms_rows(y, g_ref[...])
        if tiles_per_seq:
            first = pl.program_id(0) % tiles_per_seq == 0
            lo = max(a, N_META)
            if a == 0:
                o_ref[0:N_META, :] = (jnp.where(first, meta_ref[...], h_ref[0:N_META, :])
                                      + rn[0:N_META])
            o_ref[lo:b, :] = (jnp.where(first, h_ref[lo - N_META:b - N_META, :], h_ref[lo:b, :])
                              + rn[lo - a:])
        else:
            o_ref[a:b, :] = h_ref[a:b, :] + rn


def _outproj(mix_parts, w, h, g, tm, *, n_tiles=None, row_offset=None, meta=None, tiles_per_seq=0,
             seq_rows=None):
    n_parts = len(mix_parts)
    kw = w.shape[0] // n_parts
    if n_tiles is None:
        n_tiles = mix_parts[0].shape[0] // tm
    if row_offset is not None:
        mix_spec = pl.BlockSpec((pl.Element(tm), pl.Element(kw)), lambda i: (row_offset(i), 0))
        h_specs = [pl.BlockSpec((pl.Element(tm), pl.Element(D_MODEL)),
                                lambda i: (row_offset(i), 0))]
    else:
        mix_spec = pl.BlockSpec((tm, kw), lambda i: (i, 0))
        h_specs = [pl.BlockSpec((tm, D_MODEL), lambda i: (i, 0))]
    operands = [h]
    if meta is not None:
        h_specs = [
            pl.BlockSpec((pl.Element(tm), pl.Element(D_MODEL)),
                         lambda i: (_seq_tile_offset(i, tm, tiles_per_seq, seq_rows), 0)),
            pl.BlockSpec((N_META, D_MODEL), lambda i: (0, 0)),
        ]
        operands = [h, meta]
    return pl.pallas_call(
        functools.partial(_outproj_kernel, n_parts=n_parts, tiles_per_seq=tiles_per_seq),
        grid=(n_tiles,),
        in_specs=[mix_spec] * n_parts
                 + [pl.BlockSpec(w.shape, lambda i: (0, 0), pipeline_mode=pl.Buffered(1))]
                 + h_specs + [pl.BlockSpec((1, D_MODEL), lambda i: (0, 0))],
        out_specs=pl.BlockSpec((tm, D_MODEL), lambda i: (i, 0)),
        out_shape=jax.ShapeDtypeStruct((n_tiles * tm, D_MODEL), f32),
        compiler_params=_cparams(("arbitrary",)),
        name="outproj",
    )(*mix_parts, w, *operands, g)


def _pool_finish(window_sum, u, cnt, pw, scale, gate):
    d = window_sum / cnt - u
    po = jnp.dot(d.astype(bf16), pw, preferred_element_type=f32) * scale
    return (po * (gate * _sigmoid(gate))).astype(bf16)


def _pool_prompt_kernel(u_ref, halo_ref, pg_ref, pw_ref, sc_ref, o_ref, ext_ref, *, tiles_per_seq):
    tm = u_ref.shape[0]
    r = pl.program_id(0) % tiles_per_seq
    halo = halo_ref[...]
    ext_ref[0:16, :] = jnp.where(r == 0, jnp.zeros_like(halo), halo)
    ext_ref[16:, :] = u_ref[...]
    pos = r * tm + lax.broadcasted_iota(jnp.int32, (tm, 1), 0)
    for gi, w in enumerate(POOL_WINDOWS):
        cs = slice(gi * POOL_GROUP, (gi + 1) * POOL_GROUP)
        u = ext_ref[16:16 + tm, cs]
        s = u
        for t in range(1, w):
            s = s + ext_ref[16 - t:16 - t + tm, cs]
        cnt = jnp.minimum(w, pos + 1).astype(f32)
        o_ref[:, cs] = _pool_finish(s, u, cnt, pw_ref[gi], sc_ref[:, cs], pg_ref[:, cs])


def _pool_prompt(z4, pool_w, pool_scale, tm, tiles_per_seq):
    m = z4.shape[0]
    hb = tm // 16
    return pl.pallas_call(
        functools.partial(_pool_prompt_kernel, tiles_per_seq=tiles_per_seq),
        grid=(m // tm,),
        in_specs=[
            pl.BlockSpec((tm, 1024), lambda i: (i, 0)),
            pl.BlockSpec((16, 1024), lambda i: (jnp.maximum(i * hb - 1, 0), 0)),
            pl.BlockSpec((tm, 1024), lambda i: (i, 1)),
            pl.BlockSpec((4, POOL_GROUP, POOL_GROUP), lambda i: (0, 0, 0)),
            pl.BlockSpec((1, 1024), lambda i: (0, 0)),
        ],
        out_specs=pl.BlockSpec((tm, 1024), lambda i: (i, 0)),
        out_shape=jax.ShapeDtypeStruct((m, 1024), bf16),
        scratch_shapes=[pltpu.VMEM((tm + 16, 1024), f32)],
        compiler_params=_cparams(("arbitrary",)),
        name="pool_prompt",
    )(z4, z4, z4, pool_w, pool_scale)


def _pool_sample_kernel(u_ref, prev_ref, pg_ref, pw_ref, sc_ref, o_ref):
    for gi, w in enumerate(POOL_WINDOWS):
        cs = slice(gi * POOL_GROUP, (gi + 1) * POOL_GROUP)
        u = u_ref[:, cs]
        s = u
        for t in range(1, w):
            base = (POOL_CTX - t) * 1024 + gi * POOL_GROUP
            s = s + prev_ref[:, base:base + POOL_GROUP]
        o_ref[:, cs] = _pool_finish(s, u, float(w), pw_ref[gi], sc_ref[:, cs], pg_ref[:, cs])


def _pool_sample(z4, prev_flat, pool_w, pool_scale):
    m = z4.shape[0]
    return pl.pallas_call(
        _pool_sample_kernel,
        grid=(1,),
        in_specs=[
            pl.BlockSpec((m, 1024), lambda i: (0, 0)),
            pl.BlockSpec(prev_flat.shape, lambda i: (0, 0)),
            pl.BlockSpec((m, 1024), lambda i: (0, 1)),
            pl.BlockSpec((4, POOL_GROUP, POOL_GROUP), lambda i: (0, 0, 0)),
            pl.BlockSpec((1, 1024), lambda i: (0, 0)),
        ],
        out_specs=pl.BlockSpec((m, 1024), lambda i: (0, 0)),
        out_shape=jax.ShapeDtypeStruct((m, 1024), bf16),
        compiler_params=_cparams(("arbitrary",)),
        name="pool_sample",
    )(z4, prev_flat, z4, pool_w, pool_scale)


def _lambda(lq1, lk1, lq2, lk2):
    a = jnp.sum(lq1[...] * lk1[...], axis=-1, keepdims=True)
    b = jnp.sum(lq2[...] * lk2[...], axis=-1, keepdims=True)
    return jnp.exp(a) - jnp.exp(b) + LAMBDA_INIT


def _attn_finish(o, sub, gate):
    on = _rms_rows(o, sub) * (1.0 - LAMBDA_INIT)
    return (on * (gate * _sigmoid(gate))).astype(bf16)


def _attn_prompt_kernel(q_ref, k_ref, v_ref, ag_ref, lq1, lk1, lq2, lk2, sub_ref, o_ref,
                        kb_ref, vb_ref):
    kb_ref[...] = k_ref[...].astype(bf16)
    vb_ref[...] = v_ref[...].astype(bf16)
    lam = _lambda(lq1, lk1, lq2, lk2)
    first_map = lax.broadcasted_iota(jnp.int32, (1, LANES), 1) < DA_QKDIM
    n_tiles = (q_ref.shape[0] - N_META) // ATTN_TQ

    def stack_maps(q):
        q = q * LOG2E
        return jnp.concatenate([jnp.where(first_map, q, 0.0), jnp.where(first_map, 0.0, q)],
                               axis=0).astype(bf16)

    def causal(t):
        row = lax.broadcasted_iota(jnp.int32, (2 * t, t), 0)
        col = lax.broadcasted_iota(jnp.int32, (2 * t, t), 1)
        return jnp.where(row >= t, row - t, row) >= col

    def step(qq, carry, kt, vt, mask):
        m, l, acc = carry
        s = lax.dot_general(qq, kt, NT_DIMS, preferred_element_type=f32)
        if mask is not None:
            s = jnp.where(mask, s, -jnp.inf)
        m_new = jnp.maximum(m, jnp.max(s, axis=-1, keepdims=True))
        alpha = jnp.exp2(m - m_new)
        p = jnp.exp2(s - m_new)
        l = alpha * l + jnp.sum(p, axis=-1, keepdims=True)
        acc = alpha * acc + jnp.dot(p.astype(bf16), vt, preferred_element_type=f32)
        return m_new, l, acc

    def init(t):
        return (jnp.full((2 * t, 1), -jnp.inf, f32), jnp.zeros((2 * t, 1), f32),
                jnp.zeros((2 * t, LANES), f32))

    def finish(carry, t, rows):
        _, l, acc = carry
        o = acc[:t] / l[:t] - lam * (acc[t:] / l[t:])
        o_ref[rows, :] = _attn_finish(o, sub_ref[...], ag_ref[rows, :])

    meta = slice(0, N_META)
    qq = stack_maps(q_ref[meta, :])
    carry = step(qq, init(N_META), kb_ref[meta, :], vb_ref[meta, :], causal(N_META))
    finish(carry, N_META, meta)

    diag_mask = causal(ATTN_TQ)

    for i in range(n_tiles):
        rows = slice(N_META + i * ATTN_TQ, N_META + (i + 1) * ATTN_TQ)
        qq = stack_maps(q_ref[rows, :])
        carry = step(qq, init(ATTN_TQ), kb_ref[meta, :], vb_ref[meta, :], None)
        for j in range(i):
            cols = slice(N_META + j * ATTN_TQ, N_META + (j + 1) * ATTN_TQ)
            carry = step(qq, carry, kb_ref[cols, :], vb_ref[cols, :], None)
        carry = step(qq, carry, kb_ref[rows, :], vb_ref[rows, :], diag_mask)
        finish(carry, ATTN_TQ, rows)


def _attn_prompt(z4, k, v, lams, subln, n_seq, seq_len):
    m = z4.shape[0]
    vec = pl.BlockSpec((1, DA_QKDIM), lambda b, h: (0, 0))
    return pl.pallas_call(
        _attn_prompt_kernel,
        grid=(n_seq, DA_HEADS),
        in_specs=[
            pl.BlockSpec((seq_len, LANES), lambda b, h: (b, 16 + h)),
            pl.BlockSpec((seq_len, LANES), lambda b, h: (b, h)),
            pl.BlockSpec((seq_len, LANES), lambda b, h: (b, h)),
            pl.BlockSpec((seq_len, LANES), lambda b, h: (b, 24 + h)),
            vec, vec, vec, vec,
            pl.BlockSpec((1, LANES), lambda b, h: (0, 0)),
        ],
        out_specs=pl.BlockSpec((seq_len, LANES), lambda b, h: (b, h)),
        out_shape=jax.ShapeDtypeStruct((m, 1024), bf16),
        scratch_shapes=[pltpu.VMEM((seq_len, LANES), bf16), pltpu.VMEM((seq_len, LANES), bf16)],
        compiler_params=_cparams(("arbitrary", "arbitrary")),
        name="attn_prompt",
    )(z4, k, v, z4, *lams, subln)


def _half_sum_matrix():
    d = np.arange(LANES)[:, None] < DA_QKDIM
    c = np.arange(LANES)[None, :] < DA_QKDIM
    return (d == c).astype(np.float32)


def _attn_sample_step(p, n_p, q_ref, kn_ref, vn_ref, ag_ref, lams, sub_ref, e_ref, kp_refs, vp_refs,
                      o_ref, m_ref, l_ref, acc_ref, accr_ref):
    pages = len(kp_refs)
    lq1, lk1, lq2, lk2 = lams
    q = q_ref[0] * LOG2E
    first_half = lax.broadcasted_iota(jnp.int32, (DA_HEADS, LANES), 1) < DA_QKDIM
    swap = lambda a: pltpu.roll(a, DA_QKDIM, a.ndim - 1)

    @pl.when(p == 0)
    def _():
        prod = kn_ref[0] * q
        s0 = jnp.sum(jnp.where(first_half, prod, 0.0), axis=-1, keepdims=True)
        s1 = jnp.sum(jnp.where(first_half, 0.0, prod), axis=-1, keepdims=True)
        m_ref[...] = jnp.where(first_half, s0, s1)
        l_ref[...] = jnp.ones_like(l_ref)
        acc_ref[...] = vn_ref[0]
        accr_ref[...] = swap(vn_ref[0])

    m, l, acc, acc_r = m_ref[...], l_ref[...], acc_ref[...], accr_ref[...]
    for pg in range(pages):
        for st in range(PAGE // ATTN_SUB):
            rows = slice(st * ATTN_SUB, (st + 1) * ATTN_SUB)
            kp = kp_refs[pg][0, rows]
            vp = vp_refs[pg][0, rows]
            prod = (kp * q[None]).reshape(ATTN_SUB * DA_HEADS, LANES).astype(bf16)
            r = jnp.dot(prod, e_ref[...], preferred_element_type=f32)
            r = r.reshape(ATTN_SUB, DA_HEADS, LANES)
            m_new = jnp.maximum(m, jnp.max(r, axis=0))
            alpha = jnp.exp2(m - m_new)
            pw = jnp.exp2(r - m_new[None])
            l = alpha * l + jnp.sum(pw, axis=0)
            acc = alpha * acc + jnp.sum(pw * vp, axis=0)
            acc_r = alpha * acc_r + jnp.sum(pw * swap(vp), axis=0)
            m = m_new
    m_ref[...], l_ref[...], acc_ref[...], accr_ref[...] = m, l, acc, acc_r

    @pl.when(p == n_p - 1)
    def _():
        lam = _lambda(lq1, lk1, lq2, lk2)
        other = swap(acc_r)
        l_sw = swap(l)
        o0 = jnp.where(first_half, acc, other) / jnp.where(first_half, l, l_sw)
        o1 = jnp.where(first_half, other, acc) / jnp.where(first_half, l_sw, l)
        o_ref[0] = _attn_finish(o0 - lam * o1, sub_ref[...], ag_ref[0])


def _inproj1_attn_kernel(pt_ref, x_ref, g_ref, w_ref, wlr_ref, q_ref, kn_ref, vn_ref, ag_ref,
                         lq1, lk1, lq2, lk2, sub_ref, e_ref, *rest, pages, steps_per_seq, n_seq,
                         q_tiles):
    del pt_ref
    kp_refs, vp_refs = rest[:pages], rest[pages:2 * pages]
    z_ref, lr_ref, o_ref, xn_ref, m_ref, l_ref, acc_ref, accr_ref = rest[2 * pages:]
    i, j = pl.program_id(0), pl.program_id(1)
    s = i * pl.num_programs(1) + j

    @pl.when(j == 0)
    def _():
        xn_ref[...] = _rms_rows(x_ref[...], g_ref[...]).astype(bf16)
        lr_ref[...] = jnp.dot(xn_ref[...], wlr_ref[...], preferred_element_type=f32)

    acc = jnp.dot(xn_ref[...], w_ref[...], preferred_element_type=f32)
    z_ref[...] = acc * jnp.where(j < q_tiles, GLA_KDIM ** -0.5, 1.0).astype(f32)

    @pl.when(s < n_seq * steps_per_seq)
    def _():
        _attn_sample_step(s % steps_per_seq, steps_per_seq, q_ref, kn_ref, vn_ref, ag_ref,
                          (lq1, lk1, lq2, lk2), sub_ref, e_ref, kp_refs, vp_refs, o_ref,
                          m_ref, l_ref, acc_ref, accr_ref)


def _inproj1_attn(x, g, w, wlr, tm, tn, page_table, q, k_new, v_new, ag, lams, subln, cache_k,
                  cache_v, pages=8):
    m = x.shape[0]
    n_out = (w.shape[1] // 1024) * 1024
    ni, nj = m // tm, n_out // tn
    bd, n_pages = page_table.shape
    steps_per_seq = n_pages // pages
    assert n_pages % pages == 0 and ni * nj >= bd * steps_per_seq and 1024 % tn == 0

    def seq_step(i, j):
        s = jnp.minimum(i * nj + j, bd * steps_per_seq - 1)
        return s // steps_per_seq, s % steps_per_seq

    tok = pl.BlockSpec((1, DA_HEADS, LANES), lambda i, j, pt: (seq_step(i, j)[0], 0, 0))
    vec = pl.BlockSpec((1, DA_QKDIM), lambda i, j, pt: (0, 0))

    def page(idx):
        def index(i, j, pt):
            b, p = seq_step(i, j)
            return (pt[b * n_pages + p * pages + idx], 0, 0, 0)
        return pl.BlockSpec((1, PAGE, DA_HEADS, LANES), index)

    e = jnp.asarray(_half_sum_matrix(), dtype=bf16)
    grid_spec = pltpu.PrefetchScalarGridSpec(
        num_scalar_prefetch=1,
        grid=(ni, nj),
        in_specs=[
            pl.BlockSpec((tm, D_MODEL), lambda i, j, pt: (i, 0)),
            pl.BlockSpec((1, D_MODEL), lambda i, j, pt: (0, 0)),
            pl.BlockSpec((D_MODEL, tn), lambda i, j, pt: (0, j)),
            pl.BlockSpec((D_MODEL, LANES), lambda i, j, pt: (0, 0)),
            tok, tok, tok, tok, vec, vec, vec, vec,
            pl.BlockSpec((1, LANES), lambda i, j, pt: (0, 0)),
            pl.BlockSpec((LANES, LANES), lambda i, j, pt: (0, 0)),
        ] + [page(idx) for idx in range(pages)] * 2,
        out_specs=[
            pl.BlockSpec((tm, tn), lambda i, j, pt: (i, j)),
            pl.BlockSpec((tm, LANES), lambda i, j, pt: (i, 0)),
            tok,
        ],
        scratch_shapes=[pltpu.VMEM((tm, D_MODEL), bf16)]
                       + [pltpu.VMEM((DA_HEADS, LANES), f32) for _ in range(4)],
    )
    return pl.pallas_call(
        functools.partial(_inproj1_attn_kernel, pages=pages, steps_per_seq=steps_per_seq,
                          n_seq=bd, q_tiles=1024 // tn),
        grid_spec=grid_spec,
        out_shape=[
            jax.ShapeDtypeStruct((m, n_out), f32),
            jax.ShapeDtypeStruct((m, LANES), f32),
            jax.ShapeDtypeStruct((bd, DA_HEADS, LANES), bf16),
        ],
        compiler_params=_cparams(("arbitrary", "arbitrary")),
        name="inproj1_attn",
    )(page_table.reshape(-1), x, g, w, wlr, q, k_new, v_new, ag, *lams, subln, e,
      *([cache_k] * pages), *([cache_v] * pages))


def _gla_consts(c):
    n = int(round(math.log2(c)))
    t = np.arange(c)[:, None]
    u = np.arange(c)[None, :]
    sums = [(u <= t), (u > t)]
    masks = [(u == t)]
    for lvl in range(n):
        half = c >> (lvl + 1)
        par = 2 * half
        split = (t // par) * par + half - 1
        upper = (t % par) >= half
        sums.append((upper & (u > split) & (u <= t)) | ((~upper) & (u > t) & (u <= split)))
        masks.append(upper & ((u % par) < half) & ((u // par) == (t // par)))
    sums = np.stack(sums).astype(np.float32).reshape(-1, c)
    return np.concatenate([sums] * 3, axis=1), np.stack(masks).astype(np.float32)


def _split3(x):
    hi = x.astype(bf16)
    r = x - hi.astype(f32)
    mid = r.astype(bf16)
    lo = (r - mid.astype(f32)).astype(bf16)
    return jnp.concatenate([hi, mid, lo], axis=0)


def _log_sigmoid(x):
    return jnp.minimum(x, 0.0) - jnp.log1p(jnp.exp(-jnp.abs(x)))


def _log_decay(lr, wa, ba):
    x = jnp.dot(lr.astype(bf16), wa, preferred_element_type=f32) + ba
    return _log_sigmoid(x) * (1.0 / GLA_TAU)


def _gla_finish(o, gn, gate):
    return (_rms_rows(o, gn) * (gate * _sigmoid(gate))).astype(bf16)


def _gla_prompt_kernel(q_ref, k_ref, v_ref, gate_ref, lr_ref, wa_ref, ba_ref, gn_ref,
                       sum16_ref, msk16_ref, sum64_ref, msk64_ref, o_ref, st_ref, g_ref, s_ref):
    g_ref[...] = _log_decay(lr_ref[...], wa_ref[...], ba_ref[...]) * LOG2E

    def chunk(st, r0, c, sum_ref, msk_ref):
        n = msk_ref.shape[0] - 1
        rows = pl.ds(r0, c)
        q = q_ref[rows, :]
        k = k_ref[rows, :]
        vb = v_ref[rows, :].astype(bf16)
        e = jnp.exp2(jnp.dot(sum_ref[...], _split3(g_ref[rows, :]),
                             preferred_element_type=f32))
        o = lax.dot_general((q * e[0:c]).astype(bf16), st.astype(bf16), NT_DIMS,
                            preferred_element_type=f32)
        sc = msk_ref[0] * lax.dot_general(q.astype(bf16), k.astype(bf16), NT_DIMS,
                                          preferred_element_type=f32)
        for lvl in range(n):
            el = e[(2 + lvl) * c:(3 + lvl) * c]
            sc = sc + msk_ref[1 + lvl] * lax.dot_general(
                (q * el).astype(bf16), (k * el).astype(bf16), NT_DIMS, preferred_element_type=f32)
        o = o + jnp.dot(sc.astype(bf16), vb, preferred_element_type=f32)
        k_end = (k * e[c:2 * c]).astype(bf16)
        o_ref[rows, :] = _gla_finish(o, gn_ref[...], gate_ref[rows, :])
        return st * e[c - 1:c] + lax.dot_general(vb, k_end, TN_DIMS, preferred_element_type=f32)

    s_ref[...] = chunk(jnp.zeros(s_ref.shape, f32), 0, N_META, sum16_ref, msk16_ref)

    def body(ci, _):
        st = s_ref[...]
        for u in range(GLA_UNROLL):
            r0 = pl.multiple_of(N_META + (ci * GLA_UNROLL + u) * GLA_CHUNK, 16)
            st = chunk(st, r0, GLA_CHUNK, sum64_ref, msk64_ref)
        s_ref[...] = st
        return 0

    n_chunks = (q_ref.shape[0] - N_META) // GLA_CHUNK
    assert n_chunks % GLA_UNROLL == 0
    lax.fori_loop(0, n_chunks // GLA_UNROLL, body, 0)
    st_ref[0, 0] = s_ref[...].T


def _gla_prompt(z1, lr, wa, ba, gn, n_seq, seq_len):
    m = z1.shape[0]
    sum16, msk16 = _gla_consts(N_META)
    sum64, msk64 = _gla_consts(GLA_CHUNK)
    consts = [jnp.asarray(sum16, bf16), jnp.asarray(msk16), jnp.asarray(sum64, bf16),
              jnp.asarray(msk64)]

    def full(a):
        nd = a.ndim
        return pl.BlockSpec(a.shape, lambda b, h: (0,) * nd)

    return pl.pallas_call(
        _gla_prompt_kernel,
        grid=(n_seq, GLA_HEADS),
        in_specs=[
            pl.BlockSpec((seq_len, GLA_KDIM), lambda b, h: (b, h)),
            pl.BlockSpec((seq_len, GLA_KDIM), lambda b, h: (b, 4 + h)),
            pl.BlockSpec((seq_len, GLA_VDIM), lambda b, h: (b, 4 + h)),
            pl.BlockSpec((seq_len, GLA_VDIM), lambda b, h: (b, 8 + h)),
            pl.BlockSpec((seq_len, LANES), lambda b, h: (b, 0)),
            pl.BlockSpec((LANES, GLA_KDIM), lambda b, h: (0, h)),
            pl.BlockSpec((1, GLA_KDIM), lambda b, h: (0, h)),
            pl.BlockSpec((1, GLA_VDIM), lambda b, h: (0, 0)),
        ] + [full(a) for a in consts],
        out_specs=[
            pl.BlockSpec((seq_len, GLA_VDIM), lambda b, h: (b, h)),
            pl.BlockSpec((1, 1, GLA_KDIM, GLA_VDIM), lambda b, h: (b, h, 0, 0)),
        ],
        out_shape=[
            jax.ShapeDtypeStruct((m, GLA_HEADS * GLA_VDIM), bf16),
            jax.ShapeDtypeStruct((n_seq, GLA_HEADS, GLA_KDIM, GLA_VDIM), f32),
        ],
        scratch_shapes=[pltpu.VMEM((seq_len, GLA_KDIM), f32), pltpu.VMEM((GLA_VDIM, GLA_KDIM), f32)],
        compiler_params=_cparams(("arbitrary", "arbitrary")),
        name="gla_prompt",
    )(z1, z1, z1, z1, lr, wa, ba, gn, *consts)


def _gla_sample_kernel(z_ref, lr_ref, wa_ref, wat_ref, ba_ref, bat_ref, gn_ref, s_ref,
                       o_ref, st_ref):
    lr = lr_ref[0].astype(bf16)
    lr_rows = jnp.broadcast_to(lr, (LANES, LANES))
    first_row = lax.broadcasted_iota(jnp.int32, (16, 1), 0) == 0
    kw, vw = GLA_HEADS * GLA_KDIM, GLA_HEADS * GLA_VDIM
    for h in range(GLA_HEADS):
        kc = slice(h * GLA_KDIM, (h + 1) * GLA_KDIM)
        vc = slice(h * GLA_VDIM, (h + 1) * GLA_VDIM)
        q = z_ref[0, :, kc]
        k = z_ref[0, :, kw + h * GLA_KDIM:kw + (h + 1) * GLA_KDIM]
        v = z_ref[0, :, 2 * kw + h * GLA_VDIM:2 * kw + (h + 1) * GLA_VDIM]
        gate = z_ref[0, :, 2 * kw + vw + h * GLA_VDIM:2 * kw + vw + (h + 1) * GLA_VDIM]
        x_row = jnp.dot(lr, wa_ref[:, kc], preferred_element_type=f32) + ba_ref[:, kc]
        x_col = lax.dot_general(wat_ref[kc, :], lr_rows, NT_DIMS,
                                preferred_element_type=f32) + bat_ref[kc, :]
        eg_row = jnp.exp(_log_sigmoid(x_row) * (1.0 / GLA_TAU))
        eg_col = jnp.exp(_log_sigmoid(x_col) * (1.0 / GLA_TAU))
        s = s_ref[0, h]
        o = jnp.dot((q * eg_row).astype(bf16), s.astype(bf16), preferred_element_type=f32)
        o = o + jnp.sum(q * k, axis=-1, keepdims=True) * v
        k16 = jnp.where(first_row, k, 0.0).astype(bf16)
        v16 = jnp.broadcast_to(v, (16, GLA_VDIM)).astype(bf16)
        st_ref[0, h] = (s * jnp.concatenate([eg_col] * (GLA_VDIM // LANES), axis=1)
                        + lax.dot_general(k16, v16, TN_DIMS, preferred_element_type=f32))
        o_ref[0, :, vc] = _gla_finish(o, gn_ref[...], gate)


def _gla_sample(z1, lr, wa, ba, gn, state):
    bd = state.shape[0]
    kw = GLA_HEADS * GLA_KDIM
    wat = wa.T
    bat = jnp.broadcast_to(ba.reshape(kw, 1), (kw, LANES))
    const = lambda a: pl.BlockSpec(a.shape, lambda b: (0,) * a.ndim)
    st_spec = pl.BlockSpec((1, GLA_HEADS, GLA_KDIM, GLA_VDIM), lambda b: (b, 0, 0, 0))
    return pl.pallas_call(
        _gla_sample_kernel,
        grid=(bd,),
        in_specs=[
            pl.BlockSpec((1, 1, z1.shape[2]), lambda b: (b, 0, 0)),
            pl.BlockSpec((1, 1, LANES), lambda b: (b, 0, 0)),
            const(wa), const(wat), const(ba), const(bat), const(gn), st_spec,
        ],
        out_specs=[
            pl.BlockSpec((1, 1, GLA_HEADS * GLA_VDIM), lambda b: (b, 0, 0)),
            st_spec,
        ],
        out_shape=[
            jax.ShapeDtypeStruct((bd, 1, GLA_HEADS * GLA_VDIM), bf16),
            jax.ShapeDtypeStruct(state.shape, f32),
        ],
        compiler_params=_cparams(("arbitrary",)),
        name="gla_sample",
    )(z1, lr, wa, wat, ba, bat, gn, state)


def _rope_tables(positions):
    half = ROT_DIM // 2
    inv = jnp.power(ROPE_THETA, -jnp.arange(half, dtype=f32) * 2.0 / ROT_DIM)
    ang = positions.astype(f32)[:, None] * inv[None, :]
    cos, sin = jnp.cos(ang), jnp.sin(ang)
    t = positions.shape[0]
    rest = DA_QKDIM - ROT_DIM
    z8 = jnp.zeros((t, half), f32)
    zr = jnp.zeros((t, rest), f32)
    c = jnp.concatenate([cos, cos, jnp.ones((t, rest), f32)], axis=1)
    s1 = jnp.concatenate([z8, sin, zr], axis=1)
    s2 = jnp.concatenate([-sin, z8, zr], axis=1)
    return tuple(jnp.concatenate([a, a], axis=1) for a in (c, s1, s2))


def kernel(x_prompt, x_sample, cache_k, cache_v, state_pool, state_gla, page_table, meta_tokens,
           pre_norm_0, post_norm_0, w_in_0, pool_w_0, pool_scale_0, lambda_q1_0, lambda_k1_0,
           lambda_q2_0, lambda_k2_0, subln_0, w_out_0,
           pre_norm_1, post_norm_1, w_in_1, gla_wa2_1, gla_ba_1, gla_norm_1, w_out_1):
    B, seq = x_prompt.shape[0], x_prompt.shape[1]
    Bd = x_sample.shape[0]
    L = seq + N_META
    assert L % ROW_TILE == 0 and (L - N_META) % ATTN_TQ == 0 and x_sample.shape[1] == 1

    row = lambda a: a.reshape(1, -1)
    w0 = w_in_0.astype(bf16)
    w1 = w_in_1.astype(bf16)
    w1_lr = jnp.pad(w_in_1[:, 6144:], ((0, 0), (0, LANES - GLA_RANK))).astype(bf16)
    wa = jnp.pad(gla_wa2_1, ((0, LANES - GLA_RANK), (0, 0))).astype(bf16)
    wo0 = w_out_0.astype(bf16)
    wo1 = w_out_1.astype(bf16)
    pw = pool_w_0.astype(bf16)
    lams = [row(a) for a in (lambda_q1_0, lambda_k1_0, lambda_q2_0, lambda_k2_0)]

    meta = meta_tokens.astype(x_prompt.dtype)
    x_p = x_prompt.reshape(B * seq, D_MODEL)
    h_s = x_sample.reshape(Bd, D_MODEL)
    tiles = L // ROW_TILE
    tabs_p = _rope_tables(jnp.arange(L, dtype=jnp.int32))
    past_len = page_table.shape[1] * PAGE
    tabs_s = _rope_tables(jnp.full((Bd,), past_len, dtype=jnp.int32))

    z4_p, k_p, v_p = _inproj0(x_p, row(pre_norm_0), w0, tabs_p, ROW_TILE, tiles, meta=meta, n_seq=B)
    mix_a = _pool_prompt(z4_p, pw, row(pool_scale_0), ROW_TILE, tiles)
    mix_b = _attn_prompt(z4_p, k_p, v_p, lams, row(subln_0), B, L)
    h_p = _outproj([mix_a, mix_b], wo0, x_p, row(post_norm_0), ROW_TILE, meta=meta,
                   tiles_per_seq=tiles, seq_rows=seq)

    z4_s, k_s, v_s = _inproj0(h_s, row(pre_norm_0), w0, tabs_s, Bd, 1)
    mix_a_s = _pool_sample(z4_s, state_pool.reshape(Bd, POOL_CTX * 1024), pw, row(pool_scale_0))
    hd = lambda a: a.reshape(Bd, DA_HEADS, LANES)

    z1_p, lr_p, mix_b_s = _inproj1_attn(
        h_p, row(pre_norm_1), w1, w1_lr, ROW_TILE, FUSED_TN, page_table, hd(z4_s[:, 2048:3072]),
        hd(k_s), hd(v_s), hd(z4_s[:, 3072:4096]), lams, row(subln_0), cache_k, cache_v)
    h_s = _outproj([mix_a_s, mix_b_s.reshape(Bd, 1024)], wo0, h_s, row(post_norm_0), Bd)

    mix1_p, gla_p = _gla_prompt(z1_p, lr_p, wa, row(gla_ba_1), row(gla_norm_1), B, L)
    per_seq = seq // OUT_TILE
    y_p = _outproj([mix1_p], wo1, h_p, row(post_norm_1), OUT_TILE, n_tiles=B * per_seq,
                   row_offset=lambda i: pl.multiple_of(
                       (i // per_seq) * L + N_META + (i % per_seq) * OUT_TILE, 16))

    z1_s, lr_s = _inproj1(h_s, row(pre_norm_1), w1, w1_lr, Bd)
    mix1_s, gla_s = _gla_sample(z1_s.reshape(Bd, 1, 6144), lr_s.reshape(Bd, 1, LANES), wa,
                                row(gla_ba_1), row(gla_norm_1), state_gla)
    h_s = _outproj([mix1_s.reshape(Bd, 2048)], wo1, h_s, row(post_norm_1), Bd)

    pool_p = lax.slice(z4_p.reshape(B, L, 4096), (0, L - POOL_CTX, 0), (B, L, 1024))
    pool_s = jnp.concatenate([state_pool[:, 1:], z4_s[:, None, :1024]], axis=1)
    return (y_p.reshape(B, seq, D_MODEL), h_s.reshape(Bd, 1, D_MODEL),
            k_p.reshape(B, L, DA_HEADS, LANES), v_p.reshape(B, L, DA_HEADS, LANES), pool_p, gla_p,
            k_s.reshape(Bd, 1, DA_HEADS, LANES), v_s.reshape(Bd, 1, DA_HEADS, LANES), pool_s, gla_s)
```

```python
import functools
import math

import numpy as np
import jax
import jax.numpy as jnp
from jax import lax
from jax.experimental import pallas as pl
from jax.experimental.pallas import tpu as pltpu

f32 = jnp.float32
bf16 = jnp.bfloat16

D_MODEL = 2048
N_META = 16
NORM_EPS = 1e-6
POOL_WINDOWS = (2, 4, 8, 16)
POOL_GROUP = 256
POOL_CTX = 15
DA_HEADS = 8
DA_VDIM = 128
DA_QKDIM = 64
DA_SCALE = DA_QKDIM ** -0.5
ROT_DIM = 16
ROPE_THETA = 500000.0
LAMBDA_INIT = 0.8 - 0.6 * math.exp(-0.3 * 0)
GLA_HEADS = 4
GLA_KDIM = 256
GLA_VDIM = 512
GLA_RANK = 16
GLA_TAU = 16.0
GLA_CHUNK = 64
GLA_UNROLL = 2
GLA_SEQS = 2
PAGE = 128

LANES = 128
ROW_TILE = 688
OUT_TILE = 512
ATTN_PAGES = 16
ATTN_TQ = 512
ATTN_SUB = 8
LOG2E = 1.4426950408889634
VMEM_LIMIT = 52 * 1024 * 1024

NT_DIMS = (((1,), (1,)), ((), ()))
TN_DIMS = (((0,), (0,)), ((), ()))


def _cparams(sem):
    return pltpu.CompilerParams(dimension_semantics=sem, vmem_limit_bytes=VMEM_LIMIT)


def _sigmoid(x):
    return 1.0 / (1.0 + jnp.exp(-x))


def _rms_rows(x, g):
    ms = jnp.mean(x * x, axis=-1, keepdims=True)
    return x * lax.rsqrt(ms + NORM_EPS) * g


def _rope_store(dst_ref, a, c, s1, s2, scale):
    for h in range(DA_HEADS):
        xh = a[:, h * LANES:(h + 1) * LANES]
        r = xh * c + pltpu.roll(xh, 8, 1) * s1 + pltpu.roll(xh, LANES - 8, 1) * s2
        dst_ref[:, h * LANES:(h + 1) * LANES] = r * scale if scale != 1.0 else r


def _seq_tile_offset(i, tm, tiles_per_seq, seq_rows):
    r = i % tiles_per_seq
    return pl.multiple_of((i // tiles_per_seq) * seq_rows + jnp.maximum(r * tm - N_META, 0), 16)


def _inproj0_kernel(*refs, tiles_per_seq):
    if tiles_per_seq:
        x_ref, meta_ref, g_ref, w_ref, c_ref, s1_ref, s2_ref, z_ref, k_ref, v_ref, xn_ref = refs
    else:
        x_ref, g_ref, w_ref, c_ref, s1_ref, s2_ref, z_ref, k_ref, v_ref, xn_ref = refs
    j = pl.program_id(1)
    tm = xn_ref.shape[0]

    @pl.when(j == 0)
    def _():
        if tiles_per_seq:
            first = pl.program_id(0) % tiles_per_seq == 0

            @pl.when(first)
            def _():
                xn_ref[0:N_META, :] = _rms_rows(meta_ref[...], g_ref[...]).astype(bf16)
                xn_ref[N_META:, :] = _rms_rows(x_ref[0:tm - N_META, :], g_ref[...]).astype(bf16)

            @pl.when(jnp.logical_not(first))
            def _():
                xn_ref[...] = _rms_rows(x_ref[...], g_ref[...]).astype(bf16)
        else:
            xn_ref[...] = _rms_rows(x_ref[...], g_ref[...]).astype(bf16)

    acc = jnp.dot(xn_ref[...], w_ref[...], preferred_element_type=f32)

    @pl.when((j == 0) | (j == 1) | (j == 3))
    def _():
        z_ref[...] = acc

    @pl.when(j == 2)
    def _():
        _rope_store(z_ref, acc, c_ref[...], s1_ref[...], s2_ref[...], DA_SCALE)

    @pl.when(j == 4)
    def _():
        _rope_store(k_ref, acc, c_ref[...], s1_ref[...], s2_ref[...], 1.0)

    @pl.when(j == 5)
    def _():
        v_ref[...] = acc


def _inproj0(x, g, w, tabs, tm, tab_blocks, meta=None, n_seq=None):
    c, s1, s2 = tabs
    tab_spec = pl.BlockSpec((tm, LANES), lambda i, j: (i % tab_blocks, 0))
    if meta is None:
        m = x.shape[0]
        x_specs = [pl.BlockSpec((tm, D_MODEL), lambda i, j: (i, 0))]
        operands = (x,)
    else:
        seq_rows = x.shape[0] // n_seq
        m = x.shape[0] + n_seq * N_META
        x_specs = [
            pl.BlockSpec((pl.Element(tm), pl.Element(D_MODEL)),
                         lambda i, j: (_seq_tile_offset(i, tm, tab_blocks, seq_rows), 0)),
            pl.BlockSpec((N_META, D_MODEL), lambda i, j: (0, 0)),
        ]
        operands = (x, meta)
    return pl.pallas_call(
        functools.partial(_inproj0_kernel, tiles_per_seq=tab_blocks if meta is not None else 0),
        grid=(m // tm, 6),
        in_specs=x_specs + [
            pl.BlockSpec((1, D_MODEL), lambda i, j: (0, 0)),
            pl.BlockSpec((D_MODEL, 1024),
                         lambda i, j: (0, jnp.where(j < 3, j, jnp.where(j == 3, 5, j - 1)))),
            tab_spec, tab_spec, tab_spec,
        ],
        out_specs=[
            pl.BlockSpec((tm, 1024), lambda i, j: (i, jnp.minimum(j, 3))),
            pl.BlockSpec((tm, 1024), lambda i, j: (i, 0)),
            pl.BlockSpec((tm, 1024), lambda i, j: (i, 0)),
        ],
        out_shape=[
            jax.ShapeDtypeStruct((m, 4096), f32),
            jax.ShapeDtypeStruct((m, 1024), f32),
            jax.ShapeDtypeStruct((m, 1024), f32),
        ],
        scratch_shapes=[pltpu.VMEM((tm, D_MODEL), bf16)],
        compiler_params=_cparams(("arbitrary", "arbitrary")),
        name="inproj0",
    )(*operands, g, w, c, s1, s2)


def _inproj1_kernel(x_ref, g_ref, w_ref, wlr_ref, z_ref, lr_ref, xn_ref):
    j = pl.program_id(1)
    kw = GLA_HEADS * GLA_KDIM

    @pl.when(j == 0)
    def _():
        xn_ref[...] = _rms_rows(x_ref[...], g_ref[...]).astype(bf16)
        lr_ref[...] = jnp.dot(xn_ref[...], wlr_ref[...], preferred_element_type=f32)

    acc = jnp.dot(xn_ref[...], w_ref[...], preferred_element_type=f32)

    @pl.when(j == 0)
    def _():
        z_ref[:, :kw] = acc[:, :kw] * (GLA_KDIM ** -0.5)
        z_ref[:, kw:] = acc[:, kw:]

    @pl.when(j != 0)
    def _():
        z_ref[...] = acc


def _inproj1(x, g, w, wlr, tm):
    m = x.shape[0]
    tn = 2 * GLA_HEADS * GLA_KDIM
    return pl.pallas_call(
        _inproj1_kernel,
        grid=(m // tm, 6144 // tn),
        in_specs=[
            pl.BlockSpec((tm, D_MODEL), lambda i, j: (i, 0)),
            pl.BlockSpec((1, D_MODEL), lambda i, j: (0, 0)),
            pl.BlockSpec((D_MODEL, tn), lambda i, j: (0, j)),
            pl.BlockSpec((D_MODEL, LANES), lambda i, j: (0, 0)),
        ],
        out_specs=[
            pl.BlockSpec((tm, tn), lambda i, j: (i, j)),
            pl.BlockSpec((tm, LANES), lambda i, j: (i, 0)),
        ],
        out_shape=[
            jax.ShapeDtypeStruct((m, 6144), f32),
            jax.ShapeDtypeStruct((m, LANES), f32),
        ],
        scratch_shapes=[pltpu.VMEM((tm, D_MODEL), bf16)],
        compiler_params=_cparams(("arbitrary", "arbitrary")),
        name="inproj1",
    )(x, g, w, wlr)


def _outproj_kernel(*refs, n_parts, tiles_per_seq):
    mix_refs = refs[:n_parts]
    if tiles_per_seq:
        w_ref, h_ref, meta_ref, g_ref, o_ref = refs[n_parts:]
    else:
        w_ref, h_ref, g_ref, o_ref = refs[n_parts:]
    kw = w_ref.shape[0] // n_parts
    tm = o_ref.shape[0]
    split = (tm // 32) * 16 if tm >= 256 else tm
    for a, b in ((0, split), (split, tm)):
        if a == b:
            continue
        y = None
        for p in range(n_parts):
            t = jnp.dot(mix_refs[p][a:b, :], w_ref[p * kw:(p + 1) * kw, :],
                        preferred_element_type=f32)
            y = t if y is None else y + t
        rn = _rms_rows(y, g_ref[...])
        if tiles_per_seq:
            first = pl.program_id(0) % tiles_per_seq == 0
            lo = max(a, N_META)
            if a == 0:
                o_ref[0:N_META, :] = (jnp.where(first, meta_ref[...], h_ref[0:N_META, :])
                                      + rn[0:N_META])
            o_ref[lo:b, :] = (jnp.where(first, h_ref[lo - N_META:b - N_META, :], h_ref[lo:b, :])
                              + rn[lo - a:])
        else:
            o_ref[a:b, :] = h_ref[a:b, :] + rn


def _outproj(mix_parts, w, h, g, tm, *, n_tiles=None, row_offset=None, meta=None, tiles_per_seq=0,
             seq_rows=None):
    n_parts = len(mix_parts)
    kw = w.shape[0] // n_parts
    if n_tiles is None:
        n_tiles = mix_parts[0].shape[0] // tm
    if row_offset is not None:
        mix_spec = pl.BlockSpec((pl.Element(tm), pl.Element(kw)), lambda i: (row_offset(i), 0))
        h_specs = [pl.BlockSpec((pl.Element(tm), pl.Element(D_MODEL)),
                                lambda i: (row_offset(i), 0))]
    else:
        mix_spec = pl.BlockSpec((tm, kw), lambda i: (i, 0))
        h_specs = [pl.BlockSpec((tm, D_MODEL), lambda i: (i, 0))]
    operands = [h]
    if meta is not None:
        h_specs = [
            pl.BlockSpec((pl.Element(tm), pl.Element(D_MODEL)),
                         lambda i: (_seq_tile_offset(i, tm, tiles_per_seq, seq_rows), 0)),
            pl.BlockSpec((N_META, D_MODEL), lambda i: (0, 0)),
        ]
        operands = [h, meta]
    return pl.pallas_call(
        functools.partial(_outproj_kernel, n_parts=n_parts, tiles_per_seq=tiles_per_seq),
        grid=(n_tiles,),
        in_specs=[mix_spec] * n_parts
                 + [pl.BlockSpec(w.shape, lambda i: (0, 0), pipeline_mode=pl.Buffered(1))]
                 + h_specs + [pl.BlockSpec((1, D_MODEL), lambda i: (0, 0))],
        out_specs=pl.BlockSpec((tm, D_MODEL), lambda i: (i, 0)),
        out_shape=jax.ShapeDtypeStruct((n_tiles * tm, D_MODEL), f32),
        compiler_params=_cparams(("arbitrary",)),
        name="outproj",
    )(*mix_parts, w, *operands, g)


def _pool_finish(window_sum, u, cnt, pw, scale, gate):
    d = window_sum / cnt - u
    po = jnp.dot(d.astype(bf16), pw, preferred_element_type=f32) * scale
    return (po * (gate * _sigmoid(gate))).astype(bf16)


def _pool_prompt_kernel(u_ref, halo_ref, pg_ref, pw_ref, sc_ref, o_ref, ext_ref, *, tiles_per_seq):
    tm = u_ref.shape[0]
    r = pl.program_id(0) % tiles_per_seq
    halo = halo_ref[...]
    ext_ref[0:16, :] = jnp.where(r == 0, jnp.zeros_like(halo), halo)
    ext_ref[16:, :] = u_ref[...]
    pos = r * tm + lax.broadcasted_iota(jnp.int32, (tm, 1), 0)
    for gi, w in enumerate(POOL_WINDOWS):
        cs = slice(gi * POOL_GROUP, (gi + 1) * POOL_GROUP)
        u = ext_ref[16:16 + tm, cs]
        s = u
        for t in range(1, w):
            s = s + ext_ref[16 - t:16 - t + tm, cs]
        cnt = jnp.minimum(w, pos + 1).astype(f32)
        o_ref[:, cs] = _pool_finish(s, u, cnt, pw_ref[gi], sc_ref[:, cs], pg_ref[:, cs])


def _pool_prompt(z4, pool_w, pool_scale, tm, tiles_per_seq):
    m = z4.shape[0]
    hb = tm // 16
    return pl.pallas_call(
        functools.partial(_pool_prompt_kernel, tiles_per_seq=tiles_per_seq),
        grid=(m // tm,),
        in_specs=[
            pl.BlockSpec((tm, 1024), lambda i: (i, 0)),
            pl.BlockSpec((16, 1024), lambda i: (jnp.maximum(i * hb - 1, 0), 0)),
            pl.BlockSpec((tm, 1024), lambda i: (i, 1)),
            pl.BlockSpec((4, POOL_GROUP, POOL_GROUP), lambda i: (0, 0, 0)),
            pl.BlockSpec((1, 1024), lambda i: (0, 0)),
        ],
        out_specs=pl.BlockSpec((tm, 1024), lambda i: (i, 0)),
        out_shape=jax.ShapeDtypeStruct((m, 1024), bf16),
        scratch_shapes=[pltpu.VMEM((tm + 16, 1024), f32)],
        compiler_params=_cparams(("arbitrary",)),
        name="pool_prompt",
    )(z4, z4, z4, pool_w, pool_scale)


def _pool_sample_kernel(u_ref, prev_ref, pg_ref, pw_ref, sc_ref, o_ref):
    for gi, w in enumerate(POOL_WINDOWS):
        cs = slice(gi * POOL_GROUP, (gi + 1) * POOL_GROUP)
        u = u_ref[:, cs]
        s = u
        for t in range(1, w):
            base = (POOL_CTX - t) * 1024 + gi * POOL_GROUP
            s = s + prev_ref[:, base:base + POOL_GROUP]
        o_ref[:, cs] = _pool_finish(s, u, float(w), pw_ref[gi], sc_ref[:, cs], pg_ref[:, cs])


def _pool_sample(z4, prev_flat, pool_w, pool_scale):
    m = z4.shape[0]
    return pl.pallas_call(
        _pool_sample_kernel,
        grid=(1,),
        in_specs=[
            pl.BlockSpec((m, 1024), lambda i: (0, 0)),
            pl.BlockSpec(prev_flat.shape, lambda i: (0, 0)),
            pl.BlockSpec((m, 1024), lambda i: (0, 1)),
            pl.BlockSpec((4, POOL_GROUP, POOL_GROUP), lambda i: (0, 0, 0)),
            pl.BlockSpec((1, 1024), lambda i: (0, 0)),
        ],
        out_specs=pl.BlockSpec((m, 1024), lambda i: (0, 0)),
        out_shape=jax.ShapeDtypeStruct((m, 1024), bf16),
        compiler_params=_cparams(("arbitrary",)),
        name="pool_sample",
    )(z4, prev_flat, z4, pool_w, pool_scale)


def _lambda(lq1, lk1, lq2, lk2):
    a = jnp.sum(lq1[...] * lk1[...], axis=-1, keepdims=True)
    b = jnp.sum(lq2[...] * lk2[...], axis=-1, keepdims=True)
    return jnp.exp(a) - jnp.exp(b) + LAMBDA_INIT


def _attn_finish(o, sub, gate):
    on = _rms_rows(o, sub) * (1.0 - LAMBDA_INIT)
    return (on * (gate * _sigmoid(gate))).astype(bf16)


def _attn_prompt_kernel(q_ref, k_ref, v_ref, ag_ref, lq1, lk1, lq2, lk2, sub_ref, o_ref,
                        kb_ref, vb_ref):
    kb_ref[...] = k_ref[...].astype(bf16)
    vb_ref[...] = v_ref[...].astype(bf16)
    lam = _lambda(lq1, lk1, lq2, lk2)
    first_map = lax.broadcasted_iota(jnp.int32, (1, LANES), 1) < DA_QKDIM
    n_tiles = (q_ref.shape[0] - N_META) // ATTN_TQ

    def stack_maps(q):
        q = q * LOG2E
        return jnp.concatenate([jnp.where(first_map, q, 0.0), jnp.where(first_map, 0.0, q)],
                               axis=0).astype(bf16)

    def causal(t):
        row = lax.broadcasted_iota(jnp.int32, (2 * t, t), 0)
        col = lax.broadcasted_iota(jnp.int32, (2 * t, t), 1)
        return jnp.where(row >= t, row - t, row) >= col

    def step(qq, carry, kt, vt, mask):
        m, l, acc = carry
        s = lax.dot_general(qq, kt, NT_DIMS, preferred_element_type=f32)
        if mask is not None:
            s = jnp.where(mask, s, -jnp.inf)
        m_new = jnp.maximum(m, jnp.max(s, axis=-1, keepdims=True))
        alpha = jnp.exp2(m - m_new)
        p = jnp.exp2(s - m_new)
        l = alpha * l + jnp.sum(p, axis=-1, keepdims=True)
        acc = alpha * acc + jnp.dot(p.astype(bf16), vt, preferred_element_type=f32)
        return m_new, l, acc

    def init(t):
        return (jnp.full((2 * t, 1), -jnp.inf, f32), jnp.zeros((2 * t, 1), f32),
                jnp.zeros((2 * t, LANES), f32))

    def finish(carry, t, rows):
        _, l, acc = carry
        o = acc[:t] / l[:t] - lam * (acc[t:] / l[t:])
        o_ref[rows, :] = _attn_finish(o, sub_ref[...], ag_ref[rows, :])

    meta = slice(0, N_META)
    qq = stack_maps(q_ref[meta, :])
    carry = step(qq, init(N_META), kb_ref[meta, :], vb_ref[meta, :], causal(N_META))
    finish(carry, N_META, meta)

    diag_mask = causal(ATTN_TQ)

    for i in range(n_tiles):
        rows = slice(N_META + i * ATTN_TQ, N_META + (i + 1) * ATTN_TQ)
        qq = stack_maps(q_ref[rows, :])
        carry = step(qq, init(ATTN_TQ), kb_ref[meta, :], vb_ref[meta, :], None)
        for j in range(i):
            cols = slice(N_META + j * ATTN_TQ, N_META + (j + 1) * ATTN_TQ)
            carry = step(qq, carry, kb_ref[cols, :], vb_ref[cols, :], None)
        carry = step(qq, carry, kb_ref[rows, :], vb_ref[rows, :], diag_mask)
        finish(carry, ATTN_TQ, rows)


def _attn_prompt(z4, k, v, lams, subln, n_seq, seq_len):
    m = z4.shape[0]
    vec = pl.BlockSpec((1, DA_QKDIM), lambda b, h: (0, 0))
    return pl.pallas_call(
        _attn_prompt_kernel,
        grid=(n_seq, DA_HEADS),
        in_specs=[
            pl.BlockSpec((seq_len, LANES), lambda b, h: (b, 16 + h)),
            pl.BlockSpec((seq_len, LANES), lambda b, h: (b, h)),
            pl.BlockSpec((seq_len, LANES), lambda b, h: (b, h)),
            pl.BlockSpec((seq_len, LANES), lambda b, h: (b, 24 + h)),
            vec, vec, vec, vec,
            pl.BlockSpec((1, LANES), lambda b, h: (0, 0)),
        ],
        out_specs=pl.BlockSpec((seq_len, LANES), lambda b, h: (b, h)),
        out_shape=jax.ShapeDtypeStruct((m, 1024), bf16),
        scratch_shapes=[pltpu.VMEM((seq_len, LANES), bf16), pltpu.VMEM((seq_len, LANES), bf16)],
        compiler_params=_cparams(("arbitrary", "arbitrary")),
        name="attn_prompt",
    )(z4, k, v, z4, *lams, subln)


def _half_sum_matrix():
    d = np.arange(LANES)[:, None] < DA_QKDIM
    c = np.arange(LANES)[None, :] < DA_QKDIM
    return (d == c).astype(np.float32)


def _attn_sample_step(p, n_p, q_ref, kn_ref, vn_ref, ag_ref, lams, sub_ref, e_ref, kp_refs, vp_refs,
                      o_ref, m_ref, l_ref, acc_ref, accr_ref):
    pages = len(kp_refs)
    lq1, lk1, lq2, lk2 = lams
    q = q_ref[0] * LOG2E
    first_half = lax.broadcasted_iota(jnp.int32, (DA_HEADS, LANES), 1) < DA_QKDIM
    swap = lambda a: pltpu.roll(a, DA_QKDIM, a.ndim - 1)

    @pl.when(p == 0)
    def _():
        prod = kn_ref[0] * q
        s0 = jnp.sum(jnp.where(first_half, prod, 0.0), axis=-1, keepdims=True)
        s1 = jnp.sum(jnp.where(first_half, 0.0, prod), axis=-1, keepdims=True)
        m_ref[...] = jnp.where(first_half, s0, s1)
        l_ref[...] = jnp.ones_like(l_ref)
        acc_ref[...] = vn_ref[0]
        accr_ref[...] = swap(vn_ref[0])

    m, l, acc, acc_r = m_ref[...], l_ref[...], acc_ref[...], accr_ref[...]
    for pg in range(pages):
        for st in range(PAGE // ATTN_SUB):
            rows = slice(st * ATTN_SUB, (st + 1) * ATTN_SUB)
            kp = kp_refs[pg][0, rows]
            vp = vp_refs[pg][0, rows]
            prod = (kp * q[None]).reshape(ATTN_SUB * DA_HEADS, LANES).astype(bf16)
            r = jnp.dot(prod, e_ref[...], preferred_element_type=f32)
            r = r.reshape(ATTN_SUB, DA_HEADS, LANES)
            m_new = jnp.maximum(m, jnp.max(r, axis=0))
            alpha = jnp.exp2(m - m_new)
            pw = jnp.exp2(r - m_new[None])
            l = alpha * l + jnp.sum(pw, axis=0)
            acc = alpha * acc + jnp.sum(pw * vp, axis=0)
            acc_r = alpha * acc_r + jnp.sum(pw * swap(vp), axis=0)
            m = m_new
    m_ref[...], l_ref[...], acc_ref[...], accr_ref[...] = m, l, acc, acc_r

    @pl.when(p == n_p - 1)
    def _():
        lam = _lambda(lq1, lk1, lq2, lk2)
        other = swap(acc_r)
        l_sw = swap(l)
        o0 = jnp.where(first_half, acc, other) / jnp.where(first_half, l, l_sw)
        o1 = jnp.where(first_half, other, acc) / jnp.where(first_half, l_sw, l)
        o_ref[0] = _attn_finish(o0 - lam * o1, sub_ref[...], ag_ref[0])


def _attn_sample_kernel(pt_ref, q_ref, kn_ref, vn_ref, ag_ref, lq1, lk1, lq2, lk2, sub_ref, e_ref,
                        *rest, pages):
    del pt_ref
    _attn_sample_step(pl.program_id(1), pl.num_programs(1), q_ref, kn_ref, vn_ref, ag_ref,
                      (lq1, lk1, lq2, lk2), sub_ref, e_ref, rest[:pages], rest[pages:2 * pages],
                      *rest[2 * pages:])


def _attn_sample(page_table, q, k_new, v_new, ag, lams, subln, cache_k, cache_v, pages=ATTN_PAGES):
    bd, n_pages = page_table.shape
    assert n_pages % pages == 0
    tok = pl.BlockSpec((1, DA_HEADS, LANES), lambda b, p, pt: (b, 0, 0))
    vec = pl.BlockSpec((1, DA_QKDIM), lambda b, p, pt: (0, 0))

    def page(i):
        return pl.BlockSpec((1, PAGE, DA_HEADS, LANES),
                            lambda b, p, pt: (pt[b * n_pages + p * pages + i], 0, 0, 0))

    e = jnp.asarray(_half_sum_matrix(), dtype=bf16)
    grid_spec = pltpu.PrefetchScalarGridSpec(
        num_scalar_prefetch=1,
        grid=(bd, n_pages // pages),
        in_specs=[tok, tok, tok, tok, vec, vec, vec, vec,
                  pl.BlockSpec((1, LANES), lambda b, p, pt: (0, 0)),
                  pl.BlockSpec((LANES, LANES), lambda b, p, pt: (0, 0))]
                 + [page(i) for i in range(pages)] * 2,
        out_specs=tok,
        scratch_shapes=[pltpu.VMEM((DA_HEADS, LANES), f32) for _ in range(4)],
    )
    return pl.pallas_call(
        functools.partial(_attn_sample_kernel, pages=pages),
        grid_spec=grid_spec,
        out_shape=jax.ShapeDtypeStruct((bd, DA_HEADS, LANES), bf16),
        compiler_params=_cparams(("arbitrary", "arbitrary")),
        name="attn_sample",
    )(page_table.reshape(-1), q, k_new, v_new, ag, *lams, subln, e,
      *([cache_k] * pages), *([cache_v] * pages))


def _gla_consts(c):
    n = int(round(math.log2(c)))
    t = np.arange(c)[:, None]
    u = np.arange(c)[None, :]
    sums = [(u <= t), (u > t)]
    masks = [(u == t)]
    for lvl in range(n):
        half = c >> (lvl + 1)
        par = 2 * half
        split = (t // par) * par + half - 1
        upper = (t % par) >= half
        sums.append((upper & (u > split) & (u <= t)) | ((~upper) & (u > t) & (u <= split)))
        masks.append(upper & ((u % par) < half) & ((u // par) == (t // par)))
    sums = np.stack(sums).astype(np.float32).reshape(-1, c)
    return np.concatenate([sums] * 3, axis=1), np.stack(masks).astype(np.float32)


def _split3(x):
    hi = x.astype(bf16)
    r = x - hi.astype(f32)
    mid = r.astype(bf16)
    lo = (r - mid.astype(f32)).astype(bf16)
    return jnp.concatenate([hi, mid, lo], axis=0)


def _log_sigmoid(x):
    return jnp.minimum(x, 0.0) - jnp.log1p(jnp.exp(-jnp.abs(x)))


def _log_decay(lr, wa, ba):
    x = jnp.dot(lr.astype(bf16), wa, preferred_element_type=f32) + ba
    return _log_sigmoid(x) * (1.0 / GLA_TAU)


def _gla_finish(o, gn, gate):
    return (_rms_rows(o, gn) * (gate * _sigmoid(gate))).astype(bf16)


def _gla_prompt_kernel(q_ref, k_ref, v_ref, gate_ref, lr_ref, wa_ref, ba_ref, gn_ref,
                       sum16_ref, msk16_ref, sum64_ref, msk64_ref, o_ref, st_ref, g_ref, s_ref):
    g_ref[...] = _log_decay(lr_ref[...], wa_ref[...], ba_ref[...]) * LOG2E

    def chunk(st, r0, c, sum_ref, msk_ref):
        n = msk_ref.shape[0] - 1
        rows = pl.ds(r0, c)
        q = q_ref[rows, :]
        k = k_ref[rows, :]
        vb = v_ref[rows, :].astype(bf16)
        e = jnp.exp2(jnp.dot(sum_ref[...], _split3(g_ref[rows, :]),
                             preferred_element_type=f32))
        o = lax.dot_general((q * e[0:c]).astype(bf16), st.astype(bf16), NT_DIMS,
                            preferred_element_type=f32)
        sc = msk_ref[0] * lax.dot_general(q.astype(bf16), k.astype(bf16), NT_DIMS,
                                          preferred_element_type=f32)
        for lvl in range(n):
            el = e[(2 + lvl) * c:(3 + lvl) * c]
            sc = sc + msk_ref[1 + lvl] * lax.dot_general(
                (q * el).astype(bf16), (k * el).astype(bf16), NT_DIMS, preferred_element_type=f32)
        o = o + jnp.dot(sc.astype(bf16), vb, preferred_element_type=f32)
        k_end = (k * e[c:2 * c]).astype(bf16)
        o_ref[rows, :] = _gla_finish(o, gn_ref[...], gate_ref[rows, :])
        return st * e[c - 1:c] + lax.dot_general(vb, k_end, TN_DIMS, preferred_element_type=f32)

    s_ref[...] = chunk(jnp.zeros(s_ref.shape, f32), 0, N_META, sum16_ref, msk16_ref)

    def body(ci, _):
        st = s_ref[...]
        for u in range(GLA_UNROLL):
            r0 = pl.multiple_of(N_META + (ci * GLA_UNROLL + u) * GLA_CHUNK, 16)
            st = chunk(st, r0, GLA_CHUNK, sum64_ref, msk64_ref)
        s_ref[...] = st
        return 0

    n_chunks = (q_ref.shape[0] - N_META) // GLA_CHUNK
    assert n_chunks % GLA_UNROLL == 0
    lax.fori_loop(0, n_chunks // GLA_UNROLL, body, 0)
    st_ref[0, 0] = s_ref[...].T


def _gla_prompt(z1, lr, wa, ba, gn, n_seq, seq_len):
    m = z1.shape[0]
    sum16, msk16 = _gla_consts(N_META)
    sum64, msk64 = _gla_consts(GLA_CHUNK)
    consts = [jnp.asarray(sum16, bf16), jnp.asarray(msk16), jnp.asarray(sum64, bf16),
              jnp.asarray(msk64)]

    def full(a):
        nd = a.ndim
        return pl.BlockSpec(a.shape, lambda b, h: (0,) * nd)

    return pl.pallas_call(
        _gla_prompt_kernel,
        grid=(n_seq, GLA_HEADS),
        in_specs=[
            pl.BlockSpec((seq_len, GLA_KDIM), lambda b, h: (b, h)),
            pl.BlockSpec((seq_len, GLA_KDIM), lambda b, h: (b, 4 + h)),
            pl.BlockSpec((seq_len, GLA_VDIM), lambda b, h: (b, 4 + h)),
            pl.BlockSpec((seq_len, GLA_VDIM), lambda b, h: (b, 8 + h)),
            pl.BlockSpec((seq_len, LANES), lambda b, h: (b, 0)),
            pl.BlockSpec((LANES, GLA_KDIM), lambda b, h: (0, h)),
            pl.BlockSpec((1, GLA_KDIM), lambda b, h: (0, h)),
            pl.BlockSpec((1, GLA_VDIM), lambda b, h: (0, 0)),
        ] + [full(a) for a in consts],
        out_specs=[
            pl.BlockSpec((seq_len, GLA_VDIM), lambda b, h: (b, h)),
            pl.BlockSpec((1, 1, GLA_KDIM, GLA_VDIM), lambda b, h: (b, h, 0, 0)),
        ],
        out_shape=[
            jax.ShapeDtypeStruct((m, GLA_HEADS * GLA_VDIM), bf16),
            jax.ShapeDtypeStruct((n_seq, GLA_HEADS, GLA_KDIM, GLA_VDIM), f32),
        ],
        scratch_shapes=[pltpu.VMEM((seq_len, GLA_KDIM), f32), pltpu.VMEM((GLA_VDIM, GLA_KDIM), f32)],
        compiler_params=_cparams(("arbitrary", "arbitrary")),
        name="gla_prompt",
    )(z1, z1, z1, z1, lr, wa, ba, gn, *consts)


def _gla_sample_kernel(z_ref, lr_ref, wa_ref, wat_ref, ba_ref, bat_ref, gn_ref, s_ref,
                       o_ref, st_ref):
    first_row = lax.broadcasted_iota(jnp.int32, (16, 1), 0) == 0
    kw, vw = GLA_HEADS * GLA_KDIM, GLA_HEADS * GLA_VDIM
    for b in range(z_ref.shape[0]):
        lr = lr_ref[b].astype(bf16)
        lr_rows = jnp.broadcast_to(lr, (LANES, LANES))
        for h in range(GLA_HEADS):
            kc = slice(h * GLA_KDIM, (h + 1) * GLA_KDIM)
            vc = slice(h * GLA_VDIM, (h + 1) * GLA_VDIM)
            q = z_ref[b, :, kc]
            k = z_ref[b, :, kw + h * GLA_KDIM:kw + (h + 1) * GLA_KDIM]
            v = z_ref[b, :, 2 * kw + h * GLA_VDIM:2 * kw + (h + 1) * GLA_VDIM]
            gate = z_ref[b, :, 2 * kw + vw + h * GLA_VDIM:2 * kw + vw + (h + 1) * GLA_VDIM]
            x_row = jnp.dot(lr, wa_ref[:, kc], preferred_element_type=f32) + ba_ref[:, kc]
            x_col = lax.dot_general(wat_ref[kc, :], lr_rows, NT_DIMS,
                                    preferred_element_type=f32) + bat_ref[kc, :]
            eg_row = jnp.exp(_log_sigmoid(x_row) * (1.0 / GLA_TAU))
            eg_col = jnp.exp(_log_sigmoid(x_col) * (1.0 / GLA_TAU))
            s = s_ref[b, h]
            o = jnp.dot((q * eg_row).astype(bf16), s.astype(bf16), preferred_element_type=f32)
            o = o + jnp.sum(q * k, axis=-1, keepdims=True) * v
            k16 = jnp.where(first_row, k, 0.0).astype(bf16)
            v16 = jnp.broadcast_to(v, (16, GLA_VDIM)).astype(bf16)
            st_ref[b, h] = (s * jnp.concatenate([eg_col] * (GLA_VDIM // LANES), axis=1)
                            + lax.dot_general(k16, v16, TN_DIMS, preferred_element_type=f32))
            o_ref[b, :, vc] = _gla_finish(o, gn_ref[...], gate)


def _gla_sample(z1, lr, wa, ba, gn, state):
    bd = state.shape[0]
    assert bd % GLA_SEQS == 0
    kw = GLA_HEADS * GLA_KDIM
    wat = wa.T
    bat = jnp.broadcast_to(ba.reshape(kw, 1), (kw, LANES))
    const = lambda a: pl.BlockSpec(a.shape, lambda b: (0,) * a.ndim)
    st_spec = pl.BlockSpec((GLA_SEQS, GLA_HEADS, GLA_KDIM, GLA_VDIM), lambda b: (b, 0, 0, 0))
    return pl.pallas_call(
        _gla_sample_kernel,
        grid=(bd // GLA_SEQS,),
        in_specs=[
            pl.BlockSpec((GLA_SEQS, 1, z1.shape[2]), lambda b: (b, 0, 0)),
            pl.BlockSpec((GLA_SEQS, 1, LANES), lambda b: (b, 0, 0)),
            const(wa), const(wat), const(ba), const(bat), const(gn), st_spec,
        ],
        out_specs=[
            pl.BlockSpec((GLA_SEQS, 1, GLA_HEADS * GLA_VDIM), lambda b: (b, 0, 0)),
            st_spec,
        ],
        out_shape=[
            jax.ShapeDtypeStruct((bd, 1, GLA_HEADS * GLA_VDIM), bf16),
            jax.ShapeDtypeStruct(state.shape, f32),
        ],
        compiler_params=_cparams(("arbitrary",)),
        name="gla_sample",
    )(z1, lr, wa, wat, ba, bat, gn, state)


def _rope_tables(positions):
    half = ROT_DIM // 2
    inv = jnp.power(ROPE_THETA, -jnp.arange(half, dtype=f32) * 2.0 / ROT_DIM)
    ang = positions.astype(f32)[:, None] * inv[None, :]
    cos, sin = jnp.cos(ang), jnp.sin(ang)
    t = positions.shape[0]
    rest = DA_QKDIM - ROT_DIM
    z8 = jnp.zeros((t, half), f32)
    zr = jnp.zeros((t, rest), f32)
    c = jnp.concatenate([cos, cos, jnp.ones((t, rest), f32)], axis=1)
    s1 = jnp.concatenate([z8, sin, zr], axis=1)
    s2 = jnp.concatenate([-sin, z8, zr], axis=1)
    return tuple(jnp.concatenate([a, a], axis=1) for a in (c, s1, s2))


def kernel(x_prompt, x_sample, cache_k, cache_v, state_pool, state_gla, page_table, meta_tokens,
           pre_norm_0, post_norm_0, w_in_0, pool_w_0, pool_scale_0, lambda_q1_0, lambda_k1_0,
           lambda_q2_0, lambda_k2_0, subln_0, w_out_0,
           pre_norm_1, post_norm_1, w_in_1, gla_wa2_1, gla_ba_1, gla_norm_1, w_out_1):
    B, seq = x_prompt.shape[0], x_prompt.shape[1]
    Bd = x_sample.shape[0]
    L = seq + N_META
    assert L % ROW_TILE == 0 and (L - N_META) % ATTN_TQ == 0 and x_sample.shape[1] == 1

    row = lambda a: a.reshape(1, -1)
    w0 = w_in_0.astype(bf16)
    w1 = w_in_1.astype(bf16)
    w1_lr = jnp.pad(w_in_1[:, 6144:], ((0, 0), (0, LANES - GLA_RANK))).astype(bf16)
    wa = jnp.pad(gla_wa2_1, ((0, LANES - GLA_RANK), (0, 0))).astype(bf16)
    wo0 = w_out_0.astype(bf16)
    wo1 = w_out_1.astype(bf16)
    pw = pool_w_0.astype(bf16)
    lams = [row(a) for a in (lambda_q1_0, lambda_k1_0, lambda_q2_0, lambda_k2_0)]

    meta = meta_tokens.astype(x_prompt.dtype)
    x_p = x_prompt.reshape(B * seq, D_MODEL)
    h_s = x_sample.reshape(Bd, D_MODEL)
    tiles = L // ROW_TILE
    tabs_p = _rope_tables(jnp.arange(L, dtype=jnp.int32))
    past_len = page_table.shape[1] * PAGE
    tabs_s = _rope_tables(jnp.full((Bd,), past_len, dtype=jnp.int32))

    z4_p, k_p, v_p = _inproj0(x_p, row(pre_norm_0), w0, tabs_p, ROW_TILE, tiles, meta=meta, n_seq=B)
    mix_a = _pool_prompt(z4_p, pw, row(pool_scale_0), ROW_TILE, tiles)
    mix_b = _attn_prompt(z4_p, k_p, v_p, lams, row(subln_0), B, L)
    h_p = _outproj([mix_a, mix_b], wo0, x_p, row(post_norm_0), ROW_TILE, meta=meta,
                   tiles_per_seq=tiles, seq_rows=seq)

    z4_s, k_s, v_s = _inproj0(h_s, row(pre_norm_0), w0, tabs_s, Bd, 1)
    mix_a_s = _pool_sample(z4_s, state_pool.reshape(Bd, POOL_CTX * 1024), pw, row(pool_scale_0))
    hd = lambda a: a.reshape(Bd, DA_HEADS, LANES)

    mix_b_s = _attn_sample(page_table, hd(z4_s[:, 2048:3072]), hd(k_s), hd(v_s),
                           hd(z4_s[:, 3072:4096]), lams, row(subln_0), cache_k, cache_v)
    h_s = _outproj([mix_a_s, mix_b_s.reshape(Bd, 1024)], wo0, h_s, row(post_norm_0), Bd)

    z1_p, lr_p = _inproj1(h_p, row(pre_norm_1), w1, w1_lr, ROW_TILE)
    mix1_p, gla_p = _gla_prompt(z1_p, lr_p, wa, row(gla_ba_1), row(gla_norm_1), B, L)
    per_seq = seq // OUT_TILE
    y_p = _outproj([mix1_p], wo1, h_p, row(post_norm_1), OUT_TILE, n_tiles=B * per_seq,
                   row_offset=lambda i: pl.multiple_of(
                       (i // per_seq) * L + N_META + (i % per_seq) * OUT_TILE, 16))

    z1_s, lr_s = _inproj1(h_s, row(pre_norm_1), w1, w1_lr, Bd)
    mix1_s, gla_s = _gla_sample(z1_s.reshape(Bd, 1, 6144), lr_s.reshape(Bd, 1, LANES), wa,
                                row(gla_ba_1), row(gla_norm_1), state_gla)
    h_s = _outproj([mix1_s.reshape(Bd, 2048)], wo1, h_s, row(post_norm_1), Bd)

    pool_p = lax.slice(z4_p.reshape(B, L, 4096), (0, L - POOL_CTX, 0), (B, L, 1024))
    pool_s = jnp.concatenate([state_pool[:, 1:], z4_s[:, None, :1024]], axis=1)
    return (y_p.reshape(B, seq, D_MODEL), h_s.reshape(Bd, 1, D_MODEL),
            k_p.reshape(B, L, DA_HEADS, LANES), v_p.reshape(B, L, DA_HEADS, LANES), pool_p, gla_p,
            k_s.reshape(Bd, 1, DA_HEADS, LANES), v_s.reshape(Bd, 1, DA_HEADS, LANES), pool_s, gla_s)
```

```python
import functools
import math

import numpy as np
import jax
import jax.numpy as jnp
from jax import lax
from jax.experimental import pallas as pl
from jax.experimental.pallas import tpu as pltpu

f32 = jnp.float32
bf16 = jnp.bfloat16

D_MODEL = 2048
N_META = 16
NORM_EPS = 1e-6
POOL_WINDOWS = (2, 4, 8, 16)
POOL_GROUP = 256
POOL_CTX = 15
DA_HEADS = 8
DA_VDIM = 128
DA_QKDIM = 64
DA_SCALE = DA_QKDIM ** -0.5
ROT_DIM = 16
ROPE_THETA = 500000.0
LAMBDA_INIT = 0.8 - 0.6 * math.exp(-0.3 * 0)
GLA_HEADS = 4
GLA_KDIM = 256
GLA_VDIM = 512
GLA_RANK = 16
GLA_TAU = 16.0
GLA_CHUNK = 64
GLA_UNROLL = 2
GLA_SEQS = 2
PAGE = 128

LANES = 128
ROW_TILE = 688
OUT_TILE = 512
ATTN_PAGES = 16
ATTN_TQ = 512
ATTN_SUB = 8
LOG2E = 1.4426950408889634
VMEM_LIMIT = 52 * 1024 * 1024

NT_DIMS = (((1,), (1,)), ((), ()))
TN_DIMS = (((0,), (0,)), ((), ()))


def _cparams(sem, vmem_limit=VMEM_LIMIT):
    return pltpu.CompilerParams(dimension_semantics=sem, vmem_limit_bytes=vmem_limit)


def _sigmoid(x):
    return 1.0 / (1.0 + jnp.exp(-x))


def _rms_rows(x, g):
    ms = jnp.mean(x * x, axis=-1, keepdims=True)
    return x * lax.rsqrt(ms + NORM_EPS) * g


def _rope_store(dst_ref, a, c, s1, s2, scale):
    for h in range(DA_HEADS):
        xh = a[:, h * LANES:(h + 1) * LANES]
        r = xh * c + pltpu.roll(xh, 8, 1) * s1 + pltpu.roll(xh, LANES - 8, 1) * s2
        dst_ref[:, h * LANES:(h + 1) * LANES] = r * scale if scale != 1.0 else r


def _seq_tile_offset(i, tm, tiles_per_seq, seq_rows):
    r = i % tiles_per_seq
    return pl.multiple_of((i // tiles_per_seq) * seq_rows + jnp.maximum(r * tm - N_META, 0), 16)


def _inproj0_kernel(*refs, tiles_per_seq):
    if tiles_per_seq:
        x_ref, meta_ref, g_ref, w_ref, c_ref, s1_ref, s2_ref, z_ref, k_ref, v_ref, xn_ref = refs
    else:
        x_ref, g_ref, w_ref, c_ref, s1_ref, s2_ref, z_ref, k_ref, v_ref, xn_ref = refs
    j = pl.program_id(1)
    tm = xn_ref.shape[0]

    @pl.when(j == 0)
    def _():
        if tiles_per_seq:
            first = pl.program_id(0) % tiles_per_seq == 0

            @pl.when(first)
            def _():
                xn_ref[0:N_META, :] = _rms_rows(meta_ref[...], g_ref[...]).astype(bf16)
                xn_ref[N_META:, :] = _rms_rows(x_ref[0:tm - N_META, :], g_ref[...]).astype(bf16)

            @pl.when(jnp.logical_not(first))
            def _():
                xn_ref[...] = _rms_rows(x_ref[...], g_ref[...]).astype(bf16)
        else:
            xn_ref[...] = _rms_rows(x_ref[...], g_ref[...]).astype(bf16)

    acc = jnp.dot(xn_ref[...], w_ref[...], preferred_element_type=f32)

    @pl.when((j == 0) | (j == 1) | (j == 3))
    def _():
        z_ref[...] = acc

    @pl.when(j == 2)
    def _():
        _rope_store(z_ref, acc, c_ref[...], s1_ref[...], s2_ref[...], DA_SCALE)

    @pl.when(j == 4)
    def _():
        _rope_store(k_ref, acc, c_ref[...], s1_ref[...], s2_ref[...], 1.0)

    @pl.when(j == 5)
    def _():
        v_ref[...] = acc


def _inproj0(x, g, w, tabs, tm, tab_blocks, meta=None, n_seq=None):
    c, s1, s2 = tabs
    tab_spec = pl.BlockSpec((tm, LANES), lambda i, j: (i % tab_blocks, 0))
    if meta is None:
        m = x.shape[0]
        x_specs = [pl.BlockSpec((tm, D_MODEL), lambda i, j: (i, 0))]
        operands = (x,)
    else:
        seq_rows = x.shape[0] // n_seq
        m = x.shape[0] + n_seq * N_META
        x_specs = [
            pl.BlockSpec((pl.Element(tm), pl.Element(D_MODEL)),
                         lambda i, j: (_seq_tile_offset(i, tm, tab_blocks, seq_rows), 0)),
            pl.BlockSpec((N_META, D_MODEL), lambda i, j: (0, 0)),
        ]
        operands = (x, meta)
    return pl.pallas_call(
        functools.partial(_inproj0_kernel, tiles_per_seq=tab_blocks if meta is not None else 0),
        grid=(m // tm, 6),
        in_specs=x_specs + [
            pl.BlockSpec((1, D_MODEL), lambda i, j: (0, 0)),
            pl.BlockSpec((D_MODEL, 1024),
                         lambda i, j: (0, jnp.where(j < 3, j, jnp.where(j == 3, 5, j - 1)))),
            tab_spec, tab_spec, tab_spec,
        ],
        out_specs=[
            pl.BlockSpec((tm, 1024), lambda i, j: (i, jnp.minimum(j, 3))),
            pl.BlockSpec((tm, 1024), lambda i, j: (i, 0)),
            pl.BlockSpec((tm, 1024), lambda i, j: (i, 0)),
        ],
        out_shape=[
            jax.ShapeDtypeStruct((m, 4096), f32),
            jax.ShapeDtypeStruct((m, 1024), f32),
            jax.ShapeDtypeStruct((m, 1024), f32),
        ],
        scratch_shapes=[pltpu.VMEM((tm, D_MODEL), bf16)],
        compiler_params=_cparams(("arbitrary", "arbitrary")),
        name="inproj0",
    )(*operands, g, w, c, s1, s2)


def _inproj1_kernel(x_ref, g_ref, w_ref, wlr_ref, z_ref, lr_ref, xn_ref):
    j = pl.program_id(1)
    kw = GLA_HEADS * GLA_KDIM

    @pl.when(j == 0)
    def _():
        xn_ref[...] = _rms_rows(x_ref[...], g_ref[...]).astype(bf16)
        lr_ref[...] = jnp.dot(xn_ref[...], wlr_ref[...], preferred_element_type=f32)

    acc = jnp.dot(xn_ref[...], w_ref[...], preferred_element_type=f32)
    col = lax.broadcasted_iota(jnp.int32, (1, acc.shape[1]), 1)
    z_ref[...] = acc * jnp.where((j == 0) & (col < kw), GLA_KDIM ** -0.5, 1.0).astype(f32)


def _inproj1(x, g, w, wlr, tm):
    m = x.shape[0]
    tn = 2 * GLA_HEADS * GLA_KDIM
    return pl.pallas_call(
        _inproj1_kernel,
        grid=(m // tm, 6144 // tn),
        in_specs=[
            pl.BlockSpec((tm, D_MODEL), lambda i, j: (i, 0)),
            pl.BlockSpec((1, D_MODEL), lambda i, j: (0, 0)),
            pl.BlockSpec((D_MODEL, tn), lambda i, j: (0, j)),
            pl.BlockSpec((D_MODEL, LANES), lambda i, j: (0, 0)),
        ],
        out_specs=[
            pl.BlockSpec((tm, tn), lambda i, j: (i, j)),
            pl.BlockSpec((tm, LANES), lambda i, j: (i, 0)),
        ],
        out_shape=[
            jax.ShapeDtypeStruct((m, 6144), f32),
            jax.ShapeDtypeStruct((m, LANES), f32),
        ],
        scratch_shapes=[pltpu.VMEM((tm, D_MODEL), bf16)],
        compiler_params=_cparams(("arbitrary", "arbitrary")),
        name="inproj1",
    )(x, g, w, wlr)


def _residual_store(o_ref, h_ref, meta_ref, rn, a, b, first):
    lo = max(a, N_META)
    if a == 0:
        o_ref[0:N_META, :] = jnp.where(first, meta_ref[...], h_ref[0:N_META, :]) + rn[0:N_META]
    o_ref[lo:b, :] = (jnp.where(first, h_ref[lo - N_META:b - N_META, :], h_ref[lo:b, :])
                      + rn[lo - a:])


def _outproj_kernel(*refs, n_parts):
    mix_refs = refs[:n_parts]
    w_ref, h_ref, g_ref, o_ref = refs[n_parts:]
    kw = w_ref.shape[0] // n_parts
    y = None
    for p in range(n_parts):
        t = jnp.dot(mix_refs[p][...], w_ref[p * kw:(p + 1) * kw, :], preferred_element_type=f32)
        y = t if y is None else y + t
    o_ref[...] = h_ref[...] + _rms_rows(y, g_ref[...])


def _outproj(mix_parts, w, h, g, tm, *, n_tiles=None, row_offset=None):
    n_parts = len(mix_parts)
    kw = w.shape[0] // n_parts
    if n_tiles is None:
        n_tiles = mix_parts[0].shape[0] // tm
    if row_offset is not None:
        mix_spec = pl.BlockSpec((pl.Element(tm), pl.Element(kw)), lambda i: (row_offset(i), 0))
        h_spec = pl.BlockSpec((pl.Element(tm), pl.Element(D_MODEL)), lambda i: (row_offset(i), 0))
    else:
        mix_spec = pl.BlockSpec((tm, kw), lambda i: (i, 0))
        h_spec = pl.BlockSpec((tm, D_MODEL), lambda i: (i, 0))
    return pl.pallas_call(
        functools.partial(_outproj_kernel, n_parts=n_parts),
        grid=(n_tiles,),
        in_specs=[mix_spec] * n_parts
                 + [pl.BlockSpec(w.shape, lambda i: (0, 0), pipeline_mode=pl.Buffered(1)),
                    h_spec, pl.BlockSpec((1, D_MODEL), lambda i: (0, 0))],
        out_specs=pl.BlockSpec((tm, D_MODEL), lambda i: (i, 0)),
        out_shape=jax.ShapeDtypeStruct((n_tiles * tm, D_MODEL), f32),
        compiler_params=_cparams(("arbitrary",)),
        name="outproj",
    )(*mix_parts, w, h, g)


def _pool_finish(window_sum, u, cnt, pw, scale, gate):
    d = window_sum / cnt - u
    po = jnp.dot(d.astype(bf16), pw, preferred_element_type=f32) * scale
    return (po * (gate * _sigmoid(gate))).astype(bf16)


def _outproj0_prompt_kernel(u_ref, halo_ref, pg_ref, pw_ref, sc_ref, mixb_ref, w_ref, h_ref, meta_ref,
                            g_ref, o_ref, ext_ref, *, tiles_per_seq):
    tm = o_ref.shape[0]
    r = pl.program_id(0) % tiles_per_seq
    first = r == 0
    halo = halo_ref[...]
    ext_ref[0:16, :] = jnp.where(first, jnp.zeros_like(halo), halo)
    ext_ref[16:, :] = u_ref[...]
    pool_width = len(POOL_WINDOWS) * POOL_GROUP
    split = (tm // 32) * 16
    for a, b in ((0, split), (split, tm)):
        pos = r * tm + a + lax.broadcasted_iota(jnp.int32, (b - a, 1), 0)
        y = jnp.dot(mixb_ref[a:b, :], w_ref[pool_width:, :], preferred_element_type=f32)
        for gi, w in enumerate(POOL_WINDOWS):
            cs = slice(gi * POOL_GROUP, (gi + 1) * POOL_GROUP)
            u = ext_ref[16 + a:16 + b, cs]
            s = u
            for t in range(1, w):
                s = s + ext_ref[16 + a - t:16 + b - t, cs]
            cnt = jnp.minimum(w, pos + 1).astype(f32)
            mix = _pool_finish(s, u, cnt, pw_ref[gi], sc_ref[:, cs], pg_ref[a:b, cs])
            y = y + jnp.dot(mix, w_ref[cs, :], preferred_element_type=f32)
        rn = _rms_rows(y, g_ref[...])
        _residual_store(o_ref, h_ref, meta_ref, rn, a, b, first)


def _outproj0_prompt(z4, pool_w, pool_scale, mix_b, w, h, meta, g, tm, tiles_per_seq, seq_rows):
    m = z4.shape[0]
    hb = tm // 16
    return pl.pallas_call(
        functools.partial(_outproj0_prompt_kernel, tiles_per_seq=tiles_per_seq),
        grid=(m // tm,),
        in_specs=[
            pl.BlockSpec((tm, 1024), lambda i: (i, 0)),
            pl.BlockSpec((16, 1024), lambda i: (jnp.maximum(i * hb - 1, 0), 0)),
            pl.BlockSpec((tm, 1024), lambda i: (i, 1)),
            pl.BlockSpec((4, POOL_GROUP, POOL_GROUP), lambda i: (0, 0, 0)),
            pl.BlockSpec((1, 1024), lambda i: (0, 0)),
            pl.BlockSpec((tm, mix_b.shape[1]), lambda i: (i, 0)),
            pl.BlockSpec(w.shape, lambda i: (0, 0), pipeline_mode=pl.Buffered(1)),
            pl.BlockSpec((pl.Element(tm), pl.Element(D_MODEL)),
                         lambda i: (_seq_tile_offset(i, tm, tiles_per_seq, seq_rows), 0)),
            pl.BlockSpec((N_META, D_MODEL), lambda i: (0, 0)),
            pl.BlockSpec((1, D_MODEL), lambda i: (0, 0)),
        ],
        out_specs=pl.BlockSpec((tm, D_MODEL), lambda i: (i, 0)),
        out_shape=jax.ShapeDtypeStruct((m, D_MODEL), f32),
        scratch_shapes=[pltpu.VMEM((tm + 16, 1024), f32)],
        compiler_params=_cparams(("arbitrary",), vmem_limit=VMEM_LIMIT + 4 * 1024 * 1024),
        name="outproj0_prompt",
    )(z4, z4, z4, pool_w, pool_scale, mix_b, w, h, meta, g)


def _pool_sample_kernel(u_ref, prev_ref, pg_ref, pw_ref, sc_ref, o_ref):
    for gi, w in enumerate(POOL_WINDOWS):
        cs = slice(gi * POOL_GROUP, (gi + 1) * POOL_GROUP)
        u = u_ref[:, cs]
        s = u
        for t in range(1, w):
            base = (POOL_CTX - t) * 1024 + gi * POOL_GROUP
            s = s + prev_ref[:, base:base + POOL_GROUP]
        o_ref[:, cs] = _pool_finish(s, u, float(w), pw_ref[gi], sc_ref[:, cs], pg_ref[:, cs])


def _pool_sample(z4, prev_flat, pool_w, pool_scale):
    m = z4.shape[0]
    return pl.pallas_call(
        _pool_sample_kernel,
        grid=(1,),
        in_specs=[
            pl.BlockSpec((m, 1024), lambda i: (0, 0)),
            pl.BlockSpec(prev_flat.shape, lambda i: (0, 0)),
            pl.BlockSpec((m, 1024), lambda i: (0, 1)),
            pl.BlockSpec((4, POOL_GROUP, POOL_GROUP), lambda i: (0, 0, 0)),
            pl.BlockSpec((1, 1024), lambda i: (0, 0)),
        ],
        out_specs=pl.BlockSpec((m, 1024), lambda i: (0, 0)),
        out_shape=jax.ShapeDtypeStruct((m, 1024), bf16),
        compiler_params=_cparams(("arbitrary",)),
        name="pool_sample",
    )(z4, prev_flat, z4, pool_w, pool_scale)


def _lambda(lq1, lk1, lq2, lk2):
    a = jnp.sum(lq1[...] * lk1[...], axis=-1, keepdims=True)
    b = jnp.sum(lq2[...] * lk2[...], axis=-1, keepdims=True)
    return jnp.exp(a) - jnp.exp(b) + LAMBDA_INIT


def _attn_finish(o, sub, gate):
    on = _rms_rows(o, sub) * (1.0 - LAMBDA_INIT)
    return (on * (gate * _sigmoid(gate))).astype(bf16)


def _attn_prompt_kernel(q_ref, k_ref, v_ref, ag_ref, lq1, lk1, lq2, lk2, sub_ref, o_ref,
                        kb_ref, vb_ref):
    kb_ref[...] = k_ref[...].astype(bf16)
    vb_ref[...] = v_ref[...].astype(bf16)
    lam = _lambda(lq1, lk1, lq2, lk2)
    first_map = lax.broadcasted_iota(jnp.int32, (1, LANES), 1) < DA_QKDIM
    n_tiles = (q_ref.shape[0] - N_META) // ATTN_TQ

    def stack_maps(q):
        q = q * LOG2E
        return jnp.concatenate([jnp.where(first_map, q, 0.0), jnp.where(first_map, 0.0, q)],
                               axis=0).astype(bf16)

    def causal(t):
        row = lax.broadcasted_iota(jnp.int32, (2 * t, t), 0)
        col = lax.broadcasted_iota(jnp.int32, (2 * t, t), 1)
        return jnp.where(row >= t, row - t, row) >= col

    def step(qq, carry, kt, vt, mask):
        m, l, acc = carry
        s = lax.dot_general(qq, kt, NT_DIMS, preferred_element_type=f32)
        if mask is not None:
            s = jnp.where(mask, s, -jnp.inf)
        m_new = jnp.maximum(m, jnp.max(s, axis=-1, keepdims=True))
        alpha = jnp.exp2(m - m_new)
        p = jnp.exp2(s - m_new)
        l = alpha * l + jnp.sum(p, axis=-1, keepdims=True)
        acc = alpha * acc + jnp.dot(p.astype(bf16), vt, preferred_element_type=f32)
        return m_new, l, acc

    def init(t):
        return (jnp.full((2 * t, 1), -jnp.inf, f32), jnp.zeros((2 * t, 1), f32),
                jnp.zeros((2 * t, LANES), f32))

    def finish(carry, t, rows):
        _, l, acc = carry
        o = acc[:t] / l[:t] - lam * (acc[t:] / l[t:])
        o_ref[rows, :] = _attn_finish(o, sub_ref[...], ag_ref[rows, :])

    meta = slice(0, N_META)
    qq = stack_maps(q_ref[meta, :])
    carry = step(qq, init(N_META), kb_ref[meta, :], vb_ref[meta, :], causal(N_META))
    finish(carry, N_META, meta)

    half = ATTN_TQ // 2
    row = lax.broadcasted_iota(jnp.int32, (4 * half, half), 0)
    col = lax.broadcasted_iota(jnp.int32, (4 * half, half), 1)
    mask_first_keys = (row >= 2 * half) | (jnp.bitwise_and(row, half - 1) >= col)
    mask_second_keys = causal(half)

    for i in range(n_tiles):
        r0 = N_META + i * ATTN_TQ
        top, bot = slice(r0, r0 + half), slice(r0 + half, r0 + ATTN_TQ)
        qq = jnp.concatenate([stack_maps(q_ref[top, :]), stack_maps(q_ref[bot, :])], axis=0)
        carry = step(qq, init(ATTN_TQ), kb_ref[meta, :], vb_ref[meta, :], None)
        for j in range(i):
            cols = slice(N_META + j * ATTN_TQ, N_META + (j + 1) * ATTN_TQ)
            carry = step(qq, carry, kb_ref[cols, :], vb_ref[cols, :], None)
        m, l, acc = step(qq, carry, kb_ref[top, :], vb_ref[top, :], mask_first_keys)
        finish((m[:2 * half], l[:2 * half], acc[:2 * half]), half, top)
        carry = step(qq[2 * half:], (m[2 * half:], l[2 * half:], acc[2 * half:]),
                     kb_ref[bot, :], vb_ref[bot, :], mask_second_keys)
        finish(carry, half, bot)


def _attn_prompt(z4, k, v, lams, subln, n_seq, seq_len):
    m = z4.shape[0]
    vec = pl.BlockSpec((1, DA_QKDIM), lambda b, h: (0, 0))
    return pl.pallas_call(
        _attn_prompt_kernel,
        grid=(n_seq, DA_HEADS),
        in_specs=[
            pl.BlockSpec((seq_len, LANES), lambda b, h: (b, 16 + h)),
            pl.BlockSpec((seq_len, LANES), lambda b, h: (b, h)),
            pl.BlockSpec((seq_len, LANES), lambda b, h: (b, h)),
            pl.BlockSpec((seq_len, LANES), lambda b, h: (b, 24 + h)),
            vec, vec, vec, vec,
            pl.BlockSpec((1, LANES), lambda b, h: (0, 0)),
        ],
        out_specs=pl.BlockSpec((seq_len, LANES), lambda b, h: (b, h)),
        out_shape=jax.ShapeDtypeStruct((m, 1024), bf16),
        scratch_shapes=[pltpu.VMEM((seq_len, LANES), bf16), pltpu.VMEM((seq_len, LANES), bf16)],
        compiler_params=_cparams(("arbitrary", "arbitrary")),
        name="attn_prompt",
    )(z4, k, v, z4, *lams, subln)


def _half_sum_matrix():
    d = np.arange(LANES)[:, None] < DA_QKDIM
    c = np.arange(LANES)[None, :] < DA_QKDIM
    return (d == c).astype(np.float32)


def _attn_sample_step(p, n_p, q_ref, kn_ref, vn_ref, ag_ref, lams, sub_ref, e_ref, kp_refs, vp_refs,
                      o_ref, m_ref, l_ref, acc_ref, accr_ref):
    pages = len(kp_refs)
    lq1, lk1, lq2, lk2 = lams
    q = q_ref[0] * LOG2E
    first_half = lax.broadcasted_iota(jnp.int32, (DA_HEADS, LANES), 1) < DA_QKDIM
    swap = lambda a: pltpu.roll(a, DA_QKDIM, a.ndim - 1)

    @pl.when(p == 0)
    def _():
        prod = kn_ref[0] * q
        s0 = jnp.sum(jnp.where(first_half, prod, 0.0), axis=-1, keepdims=True)
        s1 = jnp.sum(jnp.where(first_half, 0.0, prod), axis=-1, keepdims=True)
        m_ref[...] = jnp.where(first_half, s0, s1)
        l_ref[...] = jnp.ones_like(l_ref)
        acc_ref[...] = vn_ref[0]
        accr_ref[...] = swap(vn_ref[0])

    m, l, acc, acc_r = m_ref[...], l_ref[...], acc_ref[...], accr_ref[...]
    for pg in range(pages):
        for st in range(PAGE // ATTN_SUB):
            rows = slice(st * ATTN_SUB, (st + 1) * ATTN_SUB)
            kp = kp_refs[pg][0, rows]
            vp = vp_refs[pg][0, rows]
            prod = (kp * q[None]).reshape(ATTN_SUB * DA_HEADS, LANES).astype(bf16)
            r = jnp.dot(prod, e_ref[...], preferred_element_type=f32)
            r = r.reshape(ATTN_SUB, DA_HEADS, LANES)
            m_new = jnp.maximum(m, jnp.max(r, axis=0))
            alpha = jnp.exp2(m - m_new)
            pw = jnp.exp2(r - m_new[None])
            l = alpha * l + jnp.sum(pw, axis=0)
            acc = alpha * acc + jnp.sum(pw * vp, axis=0)
            acc_r = alpha * acc_r + jnp.sum(pw * swap(vp), axis=0)
            m = m_new
    m_ref[...], l_ref[...], acc_ref[...], accr_ref[...] = m, l, acc, acc_r

    @pl.when(p == n_p - 1)
    def _():
        lam = _lambda(lq1, lk1, lq2, lk2)
        other = swap(acc_r)
        l_sw = swap(l)
        o0 = jnp.where(first_half, acc, other) / jnp.where(first_half, l, l_sw)
        o1 = jnp.where(first_half, other, acc) / jnp.where(first_half, l_sw, l)
        o_ref[0] = _attn_finish(o0 - lam * o1, sub_ref[...], ag_ref[0])


def _attn_sample_kernel(pt_ref, q_ref, kn_ref, vn_ref, ag_ref, lq1, lk1, lq2, lk2, sub_ref, e_ref,
                        *rest, pages):
    del pt_ref
    _attn_sample_step(pl.program_id(1), pl.num_programs(1), q_ref, kn_ref, vn_ref, ag_ref,
                      (lq1, lk1, lq2, lk2), sub_ref, e_ref, rest[:pages], rest[pages:2 * pages],
                      *rest[2 * pages:])


def _attn_sample(page_table, q, k_new, v_new, ag, lams, subln, cache_k, cache_v, pages=ATTN_PAGES):
    bd, n_pages = page_table.shape
    assert n_pages % pages == 0
    tok = pl.BlockSpec((1, DA_HEADS, LANES), lambda b, p, pt: (b, 0, 0))
    vec = pl.BlockSpec((1, DA_QKDIM), lambda b, p, pt: (0, 0))

    def page(i):
        return pl.BlockSpec((1, PAGE, DA_HEADS, LANES),
                            lambda b, p, pt: (pt[b * n_pages + p * pages + i], 0, 0, 0))

    e = jnp.asarray(_half_sum_matrix(), dtype=bf16)
    grid_spec = pltpu.PrefetchScalarGridSpec(
        num_scalar_prefetch=1,
        grid=(bd, n_pages // pages),
        in_specs=[tok, tok, tok, tok, vec, vec, vec, vec,
                  pl.BlockSpec((1, LANES), lambda b, p, pt: (0, 0)),
                  pl.BlockSpec((LANES, LANES), lambda b, p, pt: (0, 0))]
                 + [page(i) for i in range(pages)] * 2,
        out_specs=tok,
        scratch_shapes=[pltpu.VMEM((DA_HEADS, LANES), f32) for _ in range(4)],
    )
    return pl.pallas_call(
        functools.partial(_attn_sample_kernel, pages=pages),
        grid_spec=grid_spec,
        out_shape=jax.ShapeDtypeStruct((bd, DA_HEADS, LANES), bf16),
        compiler_params=_cparams(("arbitrary", "arbitrary")),
        name="attn_sample",
    )(page_table.reshape(-1), q, k_new, v_new, ag, *lams, subln, e,
      *([cache_k] * pages), *([cache_v] * pages))


def _gla_consts(c):
    n = int(round(math.log2(c)))
    t = np.arange(c)[:, None]
    u = np.arange(c)[None, :]
    sums = [(u <= t), (u > t)]
    masks = [(u == t)]
    for lvl in range(n):
        half = c >> (lvl + 1)
        par = 2 * half
        split = (t // par) * par + half - 1
        upper = (t % par) >= half
        sums.append((upper & (u > split) & (u <= t)) | ((~upper) & (u > t) & (u <= split)))
        masks.append(upper & ((u % par) < half) & ((u // par) == (t // par)))
    sums = np.stack(sums).astype(np.float32).reshape(-1, c)
    return np.concatenate([sums] * 3, axis=1), np.stack(masks).astype(np.float32)


def _split3(x):
    hi = x.astype(bf16)
    r = x - hi.astype(f32)
    mid = r.astype(bf16)
    lo = (r - mid.astype(f32)).astype(bf16)
    return jnp.concatenate([hi, mid, lo], axis=0)


def _log_sigmoid(x):
    return jnp.minimum(x, 0.0) - jnp.log1p(jnp.exp(-jnp.abs(x)))


def _log_decay(lr, wa, ba):
    x = jnp.dot(lr.astype(bf16), wa, preferred_element_type=f32) + ba
    return _log_sigmoid(x) * (1.0 / GLA_TAU)


def _gla_finish(o, gn, gate):
    return (_rms_rows(o, gn) * (gate * _sigmoid(gate))).astype(bf16)


def _gla_prompt_kernel(q_ref, k_ref, v_ref, gate_ref, lr_ref, wa_ref, ba_ref, gn_ref,
                       sum16_ref, msk16_ref, sum64_ref, msk64_ref, o_ref, st_ref, g_ref, s_ref):
    g_ref[...] = _log_decay(lr_ref[...], wa_ref[...], ba_ref[...]) * LOG2E

    def chunk(st, r0, c, sum_ref, msk_ref):
        n = msk_ref.shape[0] - 1
        rows = pl.ds(r0, c)
        q = q_ref[rows, :]
        k = k_ref[rows, :]
        vb = v_ref[rows, :].astype(bf16)
        e = jnp.exp2(jnp.dot(sum_ref[...], _split3(g_ref[rows, :]),
                             preferred_element_type=f32))
        o = lax.dot_general((q * e[0:c]).astype(bf16), st.astype(bf16), NT_DIMS,
                            preferred_element_type=f32)
        sc = msk_ref[0] * lax.dot_general(q.astype(bf16), k.astype(bf16), NT_DIMS,
                                          preferred_element_type=f32)
        for lvl in range(n):
            el = e[(2 + lvl) * c:(3 + lvl) * c]
            sc = sc + msk_ref[1 + lvl] * lax.dot_general(
                (q * el).astype(bf16), (k * el).astype(bf16), NT_DIMS, preferred_element_type=f32)
        o = o + jnp.dot(sc.astype(bf16), vb, preferred_element_type=f32)
        k_end = (k * e[c:2 * c]).astype(bf16)
        o_ref[rows, :] = _gla_finish(o, gn_ref[...], gate_ref[rows, :])
        return st * e[c - 1:c] + lax.dot_general(vb, k_end, TN_DIMS, preferred_element_type=f32)

    s_ref[...] = chunk(jnp.zeros(s_ref.shape, f32), 0, N_META, sum16_ref, msk16_ref)

    def body(ci, _):
        st = s_ref[...]
        for u in range(GLA_UNROLL):
            r0 = pl.multiple_of(N_META + (ci * GLA_UNROLL + u) * GLA_CHUNK, 16)
            st = chunk(st, r0, GLA_CHUNK, sum64_ref, msk64_ref)
        s_ref[...] = st
        return 0

    n_chunks = (q_ref.shape[0] - N_META) // GLA_CHUNK
    assert n_chunks % GLA_UNROLL == 0
    lax.fori_loop(0, n_chunks // GLA_UNROLL, body, 0)
    st_ref[0, 0] = s_ref[...].T


def _gla_prompt(z1, lr, wa, ba, gn, n_seq, seq_len):
    m = z1.shape[0]
    sum16, msk16 = _gla_consts(N_META)
    sum64, msk64 = _gla_consts(GLA_CHUNK)
    consts = [jnp.asarray(sum16, bf16), jnp.asarray(msk16), jnp.asarray(sum64, bf16),
              jnp.asarray(msk64)]

    def full(a):
        nd = a.ndim
        return pl.BlockSpec(a.shape, lambda b, h: (0,) * nd)

    return pl.pallas_call(
        _gla_prompt_kernel,
        grid=(n_seq, GLA_HEADS),
        in_specs=[
            pl.BlockSpec((seq_len, GLA_KDIM), lambda b, h: (b, h)),
            pl.BlockSpec((seq_len, GLA_KDIM), lambda b, h: (b, 4 + h)),
            pl.BlockSpec((seq_len, GLA_VDIM), lambda b, h: (b, 4 + h)),
            pl.BlockSpec((seq_len, GLA_VDIM), lambda b, h: (b, 8 + h)),
            pl.BlockSpec((seq_len, LANES), lambda b, h: (b, 0)),
            pl.BlockSpec((LANES, GLA_KDIM), lambda b, h: (0, h)),
            pl.BlockSpec((1, GLA_KDIM), lambda b, h: (0, h)),
            pl.BlockSpec((1, GLA_VDIM), lambda b, h: (0, 0)),
        ] + [full(a) for a in consts],
        out_specs=[
            pl.BlockSpec((seq_len, GLA_VDIM), lambda b, h: (b, h)),
            pl.BlockSpec((1, 1, GLA_KDIM, GLA_VDIM), lambda b, h: (b, h, 0, 0)),
        ],
        out_shape=[
            jax.ShapeDtypeStruct((m, GLA_HEADS * GLA_VDIM), bf16),
            jax.ShapeDtypeStruct((n_seq, GLA_HEADS, GLA_KDIM, GLA_VDIM), f32),
        ],
        scratch_shapes=[pltpu.VMEM((seq_len, GLA_KDIM), f32), pltpu.VMEM((GLA_VDIM, GLA_KDIM), f32)],
        compiler_params=_cparams(("arbitrary", "arbitrary")),
        name="gla_prompt",
    )(z1, z1, z1, z1, lr, wa, ba, gn, *consts)


def _gla_sample_kernel(z_ref, lr_ref, wa_ref, wat_ref, ba_ref, bat_ref, gn_ref, s_ref,
                       o_ref, st_ref):
    first_row = lax.broadcasted_iota(jnp.int32, (16, 1), 0) == 0
    kw, vw = GLA_HEADS * GLA_KDIM, GLA_HEADS * GLA_VDIM
    for b in range(z_ref.shape[0]):
        lr = lr_ref[b].astype(bf16)
        lr_rows = jnp.broadcast_to(lr, (LANES, LANES))
        for h in range(GLA_HEADS):
            kc = slice(h * GLA_KDIM, (h + 1) * GLA_KDIM)
            vc = slice(h * GLA_VDIM, (h + 1) * GLA_VDIM)
            q = z_ref[b, :, kc]
            k = z_ref[b, :, kw + h * GLA_KDIM:kw + (h + 1) * GLA_KDIM]
            v = z_ref[b, :, 2 * kw + h * GLA_VDIM:2 * kw + (h + 1) * GLA_VDIM]
            gate = z_ref[b, :, 2 * kw + vw + h * GLA_VDIM:2 * kw + vw + (h + 1) * GLA_VDIM]
            x_row = jnp.dot(lr, wa_ref[:, kc], preferred_element_type=f32) + ba_ref[:, kc]
            x_col = lax.dot_general(wat_ref[kc, :], lr_rows, NT_DIMS,
                                    preferred_element_type=f32) + bat_ref[kc, :]
            eg_row = jnp.exp(_log_sigmoid(x_row) * (1.0 / GLA_TAU))
            eg_col = jnp.exp(_log_sigmoid(x_col) * (1.0 / GLA_TAU))
            s = s_ref[b, h]
            o = jnp.dot((q * eg_row).astype(bf16), s.astype(bf16), preferred_element_type=f32)
            o = o + jnp.sum(q * k, axis=-1, keepdims=True) * v
            k16 = jnp.where(first_row, k, 0.0).astype(bf16)
            v16 = jnp.broadcast_to(v, (16, GLA_VDIM)).astype(bf16)
            st_ref[b, h] = (s * jnp.concatenate([eg_col] * (GLA_VDIM // LANES), axis=1)
                            + lax.dot_general(k16, v16, TN_DIMS, preferred_element_type=f32))
            o_ref[b, :, vc] = _gla_finish(o, gn_ref[...], gate)


def _gla_sample(z1, lr, wa, ba, gn, state):
    bd = state.shape[0]
    assert bd % GLA_SEQS == 0
    kw = GLA_HEADS * GLA_KDIM
    wat = wa.T
    bat = jnp.broadcast_to(ba.reshape(kw, 1), (kw, LANES))
    const = lambda a: pl.BlockSpec(a.shape, lambda b: (0,) * a.ndim)
    st_spec = pl.BlockSpec((GLA_SEQS, GLA_HEADS, GLA_KDIM, GLA_VDIM), lambda b: (b, 0, 0, 0))
    return pl.pallas_call(
        _gla_sample_kernel,
        grid=(bd // GLA_SEQS,),
        in_specs=[
            pl.BlockSpec((GLA_SEQS, 1, z1.shape[2]), lambda b: (b, 0, 0)),
            pl.BlockSpec((GLA_SEQS, 1, LANES), lambda b: (b, 0, 0)),
            const(wa), const(wat), const(ba), const(bat), const(gn), st_spec,
        ],
        out_specs=[
            pl.BlockSpec((GLA_SEQS, 1, GLA_HEADS * GLA_VDIM), lambda b: (b, 0, 0)),
            st_spec,
        ],
        out_shape=[
            jax.ShapeDtypeStruct((bd, 1, GLA_HEADS * GLA_VDIM), bf16),
            jax.ShapeDtypeStruct(state.shape, f32),
        ],
        compiler_params=_cparams(("arbitrary",)),
        name="gla_sample",
    )(z1, lr, wa, wat, ba, bat, gn, state)


def _rope_tables(positions):
    half = ROT_DIM // 2
    inv = jnp.power(ROPE_THETA, -jnp.arange(half, dtype=f32) * 2.0 / ROT_DIM)
    ang = positions.astype(f32)[:, None] * inv[None, :]
    cos, sin = jnp.cos(ang), jnp.sin(ang)
    t = positions.shape[0]
    rest = DA_QKDIM - ROT_DIM
    z8 = jnp.zeros((t, half), f32)
    zr = jnp.zeros((t, rest), f32)
    c = jnp.concatenate([cos, cos, jnp.ones((t, rest), f32)], axis=1)
    s1 = jnp.concatenate([z8, sin, zr], axis=1)
    s2 = jnp.concatenate([-sin, z8, zr], axis=1)
    return tuple(jnp.concatenate([a, a], axis=1) for a in (c, s1, s2))


def kernel(x_prompt, x_sample, cache_k, cache_v, state_pool, state_gla, page_table, meta_tokens,
           pre_norm_0, post_norm_0, w_in_0, pool_w_0, pool_scale_0, lambda_q1_0, lambda_k1_0,
           lambda_q2_0, lambda_k2_0, subln_0, w_out_0,
           pre_norm_1, post_norm_1, w_in_1, gla_wa2_1, gla_ba_1, gla_norm_1, w_out_1):
    B, seq = x_prompt.shape[0], x_prompt.shape[1]
    Bd = x_sample.shape[0]
    L = seq + N_META
    assert L % ROW_TILE == 0 and (L - N_META) % ATTN_TQ == 0 and x_sample.shape[1] == 1

    row = lambda a: a.reshape(1, -1)
    w0 = w_in_0.astype(bf16)
    w1 = w_in_1.astype(bf16)
    w1_lr = jnp.pad(w_in_1[:, 6144:], ((0, 0), (0, LANES - GLA_RANK))).astype(bf16)
    wa = jnp.pad(gla_wa2_1, ((0, LANES - GLA_RANK), (0, 0))).astype(bf16)
    wo0 = w_out_0.astype(bf16)
    wo1 = w_out_1.astype(bf16)
    pw = pool_w_0.astype(bf16)
    lams = [row(a) for a in (lambda_q1_0, lambda_k1_0, lambda_q2_0, lambda_k2_0)]

    meta = meta_tokens.astype(x_prompt.dtype)
    x_p = x_prompt.reshape(B * seq, D_MODEL)
    h_s = x_sample.reshape(Bd, D_MODEL)
    tiles = L // ROW_TILE
    tabs_p = _rope_tables(jnp.arange(L, dtype=jnp.int32))
    past_len = page_table.shape[1] * PAGE
    tabs_s = _rope_tables(jnp.full((Bd,), past_len, dtype=jnp.int32))

    z4_p, k_p, v_p = _inproj0(x_p, row(pre_norm_0), w0, tabs_p, ROW_TILE, tiles, meta=meta, n_seq=B)
    mix_b = _attn_prompt(z4_p, k_p, v_p, lams, row(subln_0), B, L)
    h_p = _outproj0_prompt(z4_p, pw, row(pool_scale_0), mix_b, wo0, x_p, meta, row(post_norm_0),
                           ROW_TILE, tiles, seq)

    z4_s, k_s, v_s = _inproj0(h_s, row(pre_norm_0), w0, tabs_s, Bd, 1)
    mix_a_s = _pool_sample(z4_s, state_pool.reshape(Bd, POOL_CTX * 1024), pw, row(pool_scale_0))
    hd = lambda a: a.reshape(Bd, DA_HEADS, LANES)

    mix_b_s = _attn_sample(page_table, hd(z4_s[:, 2048:3072]), hd(k_s), hd(v_s),
                           hd(z4_s[:, 3072:4096]), lams, row(subln_0), cache_k, cache_v)
    h_s = _outproj([mix_a_s, mix_b_s.reshape(Bd, 1024)], wo0, h_s, row(post_norm_0), Bd)

    z1_p, lr_p = _inproj1(h_p, row(pre_norm_1), w1, w1_lr, ROW_TILE)
    mix1_p, gla_p = _gla_prompt(z1_p, lr_p, wa, row(gla_ba_1), row(gla_norm_1), B, L)
    per_seq = seq // OUT_TILE
    y_p = _outproj([mix1_p], wo1, h_p, row(post_norm_1), OUT_TILE, n_tiles=B * per_seq,
                   row_offset=lambda i: pl.multiple_of(
                       (i // per_seq) * L + N_META + (i % per_seq) * OUT_TILE, 16))

    z1_s, lr_s = _inproj1(h_s, row(pre_norm_1), w1, w1_lr, Bd)
    mix1_s, gla_s = _gla_sample(z1_s.reshape(Bd, 1, 6144), lr_s.reshape(Bd, 1, LANES), wa,
                                row(gla_ba_1), row(gla_norm_1), state_gla)
    h_s = _outproj([mix1_s.reshape(Bd, 2048)], wo1, h_s, row(post_norm_1), Bd)

    pool_p = lax.slice(z4_p.reshape(B, L, 4096), (0, L - POOL_CTX, 0), (B, L, 1024))
    pool_s = jnp.concatenate([state_pool[:, 1:], z4_s[:, None, :1024]], axis=1)
    return (y_p.reshape(B, seq, D_MODEL), h_s.reshape(Bd, 1, D_MODEL),
            k_p.reshape(B, L, DA_HEADS, LANES), v_p.reshape(B, L, DA_HEADS, LANES), pool_p, gla_p,
            k_s.reshape(Bd, 1, DA_HEADS, LANES), v_s.reshape(Bd, 1, DA_HEADS, LANES), pool_s, gla_s)
```

```python
import functools
import math

import numpy as np
import jax
import jax.numpy as jnp
from jax import lax
from jax.experimental import pallas as pl
from jax.experimental.pallas import tpu as pltpu

f32 = jnp.float32
bf16 = jnp.bfloat16

D_MODEL = 2048
N_META = 16
NORM_EPS = 1e-6
POOL_WINDOWS = (2, 4, 8, 16)
POOL_GROUP = 256
POOL_CTX = 15
DA_HEADS = 8
DA_VDIM = 128
DA_QKDIM = 64
DA_SCALE = DA_QKDIM ** -0.5
ROT_DIM = 16
ROPE_THETA = 500000.0
LAMBDA_INIT = 0.8 - 0.6 * math.exp(-0.3 * 0)
GLA_HEADS = 4
GLA_KDIM = 256
GLA_VDIM = 512
GLA_RANK = 16
GLA_TAU = 16.0
GLA_CHUNK = 64
GLA_UNROLL = 2
GLA_SEQS = 2
PAGE = 128

LANES = 128
ROW_TILE = 688
OUT_TILE = 512
ATTN_PAGES = 16
ATTN_TQ = 512
ATTN_SUB = 8
LOG2E = 1.4426950408889634
VMEM_LIMIT = 52 * 1024 * 1024

NT_DIMS = (((1,), (1,)), ((), ()))
TN_DIMS = (((0,), (0,)), ((), ()))


def _cparams(sem, vmem_limit=VMEM_LIMIT):
    return pltpu.CompilerParams(dimension_semantics=sem, vmem_limit_bytes=vmem_limit)


def _sigmoid(x):
    return 1.0 / (1.0 + jnp.exp(-x))


def _rms_rows(x, g):
    ms = jnp.mean(x * x, axis=-1, keepdims=True)
    return x * lax.rsqrt(ms + NORM_EPS) * g


def _rope_store(dst_ref, a, c, s1, s2, scale):
    for h in range(a.shape[1] // LANES):
        xh = a[:, h * LANES:(h + 1) * LANES]
        r = xh * c + pltpu.roll(xh, 8, 1) * s1 + pltpu.roll(xh, LANES - 8, 1) * s2
        dst_ref[:, h * LANES:(h + 1) * LANES] = r * scale if scale != 1.0 else r


def _seq_tile_offset(i, tm, tiles_per_seq, seq_rows):
    r = i % tiles_per_seq
    return pl.multiple_of((i // tiles_per_seq) * seq_rows + jnp.maximum(r * tm - N_META, 0), 16)


def _inproj0_kernel(*refs, tiles_per_seq):
    if tiles_per_seq:
        x_ref, meta_ref, g_ref, w_ref, c_ref, s1_ref, s2_ref, z_ref, k_ref, v_ref, xn_ref = refs
    else:
        x_ref, g_ref, w_ref, c_ref, s1_ref, s2_ref, z_ref, k_ref, v_ref, xn_ref = refs
    j = pl.program_id(1)
    tm = xn_ref.shape[0]

    @pl.when(j == 0)
    def _():
        if tiles_per_seq:
            first = pl.program_id(0) % tiles_per_seq == 0

            @pl.when(first)
            def _():
                xn_ref[0:N_META, :] = _rms_rows(meta_ref[...], g_ref[...]).astype(bf16)
                xn_ref[N_META:, :] = _rms_rows(x_ref[0:tm - N_META, :], g_ref[...]).astype(bf16)

            @pl.when(jnp.logical_not(first))
            def _():
                xn_ref[...] = _rms_rows(x_ref[...], g_ref[...]).astype(bf16)
        else:
            xn_ref[...] = _rms_rows(x_ref[...], g_ref[...]).astype(bf16)

    def plain(dst_ref):
        dst_ref[...] = jnp.dot(xn_ref[...], w_ref[...], preferred_element_type=f32)

    def rotated(dst_ref, scale):
        chunk = 2 * LANES
        for ch in range(w_ref.shape[1] // chunk):
            cols = slice(ch * chunk, (ch + 1) * chunk)
            a = jnp.dot(xn_ref[...], w_ref[:, cols], preferred_element_type=f32)
            _rope_store(dst_ref.at[:, cols], a, c_ref[...], s1_ref[...], s2_ref[...], scale)

    @pl.when((j == 0) | (j == 1) | (j == 3))
    def _():
        plain(z_ref)

    @pl.when(j == 2)
    def _():
        rotated(z_ref, DA_SCALE)

    @pl.when(j == 4)
    def _():
        rotated(k_ref, 1.0)

    @pl.when(j == 5)
    def _():
        plain(v_ref)


def _inproj0(x, g, w, tabs, tm, tab_blocks, meta=None, n_seq=None):
    c, s1, s2 = tabs
    tab_spec = pl.BlockSpec((tm, LANES), lambda i, j: (i % tab_blocks, 0))
    if meta is None:
        m = x.shape[0]
        x_specs = [pl.BlockSpec((tm, D_MODEL), lambda i, j: (i, 0))]
        operands = (x,)
    else:
        seq_rows = x.shape[0] // n_seq
        m = x.shape[0] + n_seq * N_META
        x_specs = [
            pl.BlockSpec((pl.Element(tm), pl.Element(D_MODEL)),
                         lambda i, j: (_seq_tile_offset(i, tm, tab_blocks, seq_rows), 0)),
            pl.BlockSpec((N_META, D_MODEL), lambda i, j: (0, 0)),
        ]
        operands = (x, meta)
    return pl.pallas_call(
        functools.partial(_inproj0_kernel, tiles_per_seq=tab_blocks if meta is not None else 0),
        grid=(m // tm, 6),
        in_specs=x_specs + [
            pl.BlockSpec((1, D_MODEL), lambda i, j: (0, 0)),
            pl.BlockSpec((D_MODEL, 1024),
                         lambda i, j: (0, jnp.where(j < 3, j, jnp.where(j == 3, 5, j - 1)))),
            tab_spec, tab_spec, tab_spec,
        ],
        out_specs=[
            pl.BlockSpec((tm, 1024), lambda i, j: (i, jnp.minimum(j, 3))),
            pl.BlockSpec((tm, 1024), lambda i, j: (i, 0)),
            pl.BlockSpec((tm, 1024), lambda i, j: (i, 0)),
        ],
        out_shape=[
            jax.ShapeDtypeStruct((m, 4096), f32),
            jax.ShapeDtypeStruct((m, 1024), f32),
            jax.ShapeDtypeStruct((m, 1024), f32),
        ],
        scratch_shapes=[pltpu.VMEM((tm, D_MODEL), bf16)],
        compiler_params=_cparams(("arbitrary", "arbitrary")),
        name="inproj0",
    )(*operands, g, w, c, s1, s2)


def _inproj1_kernel(x_ref, g_ref, w_ref, wlr_ref, z_ref, lr_ref, xn_ref):
    j = pl.program_id(1)
    kw = GLA_HEADS * GLA_KDIM

    @pl.when(j == 0)
    def _():
        xn_ref[...] = _rms_rows(x_ref[...], g_ref[...]).astype(bf16)
        lr_ref[...] = jnp.dot(xn_ref[...], wlr_ref[...], preferred_element_type=f32)

    acc = jnp.dot(xn_ref[...], w_ref[...], preferred_element_type=f32)
    col = lax.broadcasted_iota(jnp.int32, (1, acc.shape[1]), 1)
    z_ref[...] = acc * jnp.where((j == 0) & (col < kw), GLA_KDIM ** -0.5, 1.0).astype(f32)


def _inproj1(x, g, w, wlr, tm):
    m = x.shape[0]
    tn = 2 * GLA_HEADS * GLA_KDIM
    return pl.pallas_call(
        _inproj1_kernel,
        grid=(m // tm, 6144 // tn),
        in_specs=[
            pl.BlockSpec((tm, D_MODEL), lambda i, j: (i, 0)),
            pl.BlockSpec((1, D_MODEL), lambda i, j: (0, 0)),
            pl.BlockSpec((D_MODEL, tn), lambda i, j: (0, j)),
            pl.BlockSpec((D_MODEL, LANES), lambda i, j: (0, 0)),
        ],
        out_specs=[
            pl.BlockSpec((tm, tn), lambda i, j: (i, j)),
            pl.BlockSpec((tm, LANES), lambda i, j: (i, 0)),
        ],
        out_shape=[
            jax.ShapeDtypeStruct((m, 6144), f32),
            jax.ShapeDtypeStruct((m, LANES), f32),
        ],
        scratch_shapes=[pltpu.VMEM((tm, D_MODEL), bf16)],
        compiler_params=_cparams(("arbitrary", "arbitrary")),
        name="inproj1",
    )(x, g, w, wlr)


def _residual_store(o_ref, h_ref, meta_ref, rn, a, b, first):
    lo = max(a, N_META)
    if a == 0:
        o_ref[0:N_META, :] = jnp.where(first, meta_ref[...], h_ref[0:N_META, :]) + rn[0:N_META]
    o_ref[lo:b, :] = (jnp.where(first, h_ref[lo - N_META:b - N_META, :], h_ref[lo:b, :])
                      + rn[lo - a:])


def _outproj_kernel(*refs, n_parts):
    mix_refs = refs[:n_parts]
    w_ref, h_ref, g_ref, o_ref = refs[n_parts:]
    kw = w_ref.shape[0] // n_parts
    y = None
    for p in range(n_parts):
        t = jnp.dot(mix_refs[p][...], w_ref[p * kw:(p + 1) * kw, :], preferred_element_type=f32)
        y = t if y is None else y + t
    o_ref[...] = h_ref[...] + _rms_rows(y, g_ref[...])


def _outproj(mix_parts, w, h, g, tm, *, n_tiles=None, row_offset=None):
    n_parts = len(mix_parts)
    kw = w.shape[0] // n_parts
    if n_tiles is None:
        n_tiles = mix_parts[0].shape[0] // tm
    if row_offset is not None:
        mix_spec = pl.BlockSpec((pl.Element(tm), pl.Element(kw)), lambda i: (row_offset(i), 0))
        h_spec = pl.BlockSpec((pl.Element(tm), pl.Element(D_MODEL)), lambda i: (row_offset(i), 0))
    else:
        mix_spec = pl.BlockSpec((tm, kw), lambda i: (i, 0))
        h_spec = pl.BlockSpec((tm, D_MODEL), lambda i: (i, 0))
    return pl.pallas_call(
        functools.partial(_outproj_kernel, n_parts=n_parts),
        grid=(n_tiles,),
        in_specs=[mix_spec] * n_parts
                 + [pl.BlockSpec(w.shape, lambda i: (0, 0), pipeline_mode=pl.Buffered(1)),
                    h_spec, pl.BlockSpec((1, D_MODEL), lambda i: (0, 0))],
        out_specs=pl.BlockSpec((tm, D_MODEL), lambda i: (i, 0)),
        out_shape=jax.ShapeDtypeStruct((n_tiles * tm, D_MODEL), f32),
        compiler_params=_cparams(("arbitrary",)),
        name="outproj",
    )(*mix_parts, w, h, g)


def _pool_finish(window_sum, u, cnt, pw, scale, gate):
    d = window_sum / cnt - u
    po = jnp.dot(d.astype(bf16), pw, preferred_element_type=f32) * scale
    return (po * (gate * _sigmoid(gate))).astype(bf16)


def _outproj0_prompt_kernel(u_ref, halo_ref, pg_ref, pw_ref, sc_ref, mixb_ref, w_ref, h_ref, meta_ref,
                            g_ref, o_ref, ext_ref, *, tiles_per_seq):
    tm = o_ref.shape[0]
    r = pl.program_id(0) % tiles_per_seq
    first = r == 0
    halo = halo_ref[...]
    ext_ref[0:16, :] = jnp.where(first, jnp.zeros_like(halo), halo)
    ext_ref[16:, :] = u_ref[...]
    pool_width = len(POOL_WINDOWS) * POOL_GROUP
    split = (tm // 32) * 16
    for a, b in ((0, split), (split, tm)):
        pos = r * tm + a + lax.broadcasted_iota(jnp.int32, (b - a, 1), 0)
        y = jnp.dot(mixb_ref[a:b, :], w_ref[pool_width:, :], preferred_element_type=f32)
        for gi, w in enumerate(POOL_WINDOWS):
            cs = slice(gi * POOL_GROUP, (gi + 1) * POOL_GROUP)
            u = ext_ref[16 + a:16 + b, cs]
            s = u
            for t in range(1, w):
                s = s + ext_ref[16 + a - t:16 + b - t, cs]
            cnt = jnp.minimum(w, pos + 1).astype(f32)
            mix = _pool_finish(s, u, cnt, pw_ref[gi], sc_ref[:, cs], pg_ref[a:b, cs])
            y = y + jnp.dot(mix, w_ref[cs, :], preferred_element_type=f32)
        rn = _rms_rows(y, g_ref[...])
        _residual_store(o_ref, h_ref, meta_ref, rn, a, b, first)


def _outproj0_prompt(z4, pool_w, pool_scale, mix_b, w, h, meta, g, tm, tiles_per_seq, seq_rows):
    m = z4.shape[0]
    hb = tm // 16
    return pl.pallas_call(
        functools.partial(_outproj0_prompt_kernel, tiles_per_seq=tiles_per_seq),
        grid=(m // tm,),
        in_specs=[
            pl.BlockSpec((tm, 1024), lambda i: (i, 0)),
            pl.BlockSpec((16, 1024), lambda i: (jnp.maximum(i * hb - 1, 0), 0)),
            pl.BlockSpec((tm, 1024), lambda i: (i, 1)),
            pl.BlockSpec((4, POOL_GROUP, POOL_GROUP), lambda i: (0, 0, 0)),
            pl.BlockSpec((1, 1024), lambda i: (0, 0)),
            pl.BlockSpec((tm, mix_b.shape[1]), lambda i: (i, 0)),
            pl.BlockSpec(w.shape, lambda i: (0, 0), pipeline_mode=pl.Buffered(1)),
            pl.BlockSpec((pl.Element(tm), pl.Element(D_MODEL)),
                         lambda i: (_seq_tile_offset(i, tm, tiles_per_seq, seq_rows), 0)),
            pl.BlockSpec((N_META, D_MODEL), lambda i: (0, 0)),
            pl.BlockSpec((1, D_MODEL), lambda i: (0, 0)),
        ],
        out_specs=pl.BlockSpec((tm, D_MODEL), lambda i: (i, 0)),
        out_shape=jax.ShapeDtypeStruct((m, D_MODEL), f32),
        scratch_shapes=[pltpu.VMEM((tm + 16, 1024), f32)],
        compiler_params=_cparams(("arbitrary",), vmem_limit=VMEM_LIMIT + 4 * 1024 * 1024),
        name="outproj0_prompt",
    )(z4, z4, z4, pool_w, pool_scale, mix_b, w, h, meta, g)


def _pool_sample_kernel(u_ref, prev_ref, pg_ref, pw_ref, sc_ref, o_ref):
    for gi, w in enumerate(POOL_WINDOWS):
        cs = slice(gi * POOL_GROUP, (gi + 1) * POOL_GROUP)
        u = u_ref[:, cs]
        s = u
        for t in range(1, w):
            s = s + prev_ref[:, POOL_CTX - t, cs]
        o_ref[:, cs] = _pool_finish(s, u, float(w), pw_ref[gi], sc_ref[:, cs], pg_ref[:, cs])


def _pool_sample(z4, prev, pool_w, pool_scale):
    m = z4.shape[0]
    return pl.pallas_call(
        _pool_sample_kernel,
        grid=(1,),
        in_specs=[
            pl.BlockSpec((m, 1024), lambda i: (0, 0)),
            pl.BlockSpec(prev.shape, lambda i: (0, 0, 0)),
            pl.BlockSpec((m, 1024), lambda i: (0, 1)),
            pl.BlockSpec((4, POOL_GROUP, POOL_GROUP), lambda i: (0, 0, 0)),
            pl.BlockSpec((1, 1024), lambda i: (0, 0)),
        ],
        out_specs=pl.BlockSpec((m, 1024), lambda i: (0, 0)),
        out_shape=jax.ShapeDtypeStruct((m, 1024), bf16),
        compiler_params=_cparams(("arbitrary",)),
        name="pool_sample",
    )(z4, prev, z4, pool_w, pool_scale)


def _lambda(lq1, lk1, lq2, lk2):
    a = jnp.sum(lq1[...] * lk1[...], axis=-1, keepdims=True)
    b = jnp.sum(lq2[...] * lk2[...], axis=-1, keepdims=True)
    return jnp.exp(a) - jnp.exp(b) + LAMBDA_INIT


def _attn_finish(o, sub, gate):
    on = _rms_rows(o, sub) * (1.0 - LAMBDA_INIT)
    return (on * (gate * _sigmoid(gate))).astype(bf16)


def _attn_prompt_kernel(q_ref, k_ref, v_ref, ag_ref, lq1, lk1, lq2, lk2, sub_ref, o_ref,
                        kb_ref, vb_ref):
    kb_ref[...] = k_ref[...].astype(bf16)
    vb_ref[...] = v_ref[...].astype(bf16)
    lam = _lambda(lq1, lk1, lq2, lk2)
    first_map = lax.broadcasted_iota(jnp.int32, (1, LANES), 1) < DA_QKDIM
    n_tiles = (q_ref.shape[0] - N_META) // ATTN_TQ

    def stack_maps(q):
        q = q * LOG2E
        return jnp.concatenate([jnp.where(first_map, q, 0.0), jnp.where(first_map, 0.0, q)],
                               axis=0).astype(bf16)

    def causal(t):
        row = lax.broadcasted_iota(jnp.int32, (2 * t, t), 0)
        col = lax.broadcasted_iota(jnp.int32, (2 * t, t), 1)
        return jnp.where(row >= t, row - t, row) >= col

    def step(qq, carry, kt, vt, mask):
        m, l, acc = carry
        s = lax.dot_general(qq, kt, NT_DIMS, preferred_element_type=f32)
        if mask is not None:
            s = jnp.where(mask, s, -jnp.inf)
        m_new = jnp.maximum(m, jnp.max(s, axis=-1, keepdims=True))
        alpha = jnp.exp2(m - m_new)
        p = jnp.exp2(s - m_new)
        l = alpha * l + jnp.sum(p, axis=-1, keepdims=True)
        acc = alpha * acc + jnp.dot(p.astype(bf16), vt, preferred_element_type=f32)
        return m_new, l, acc

    def init(t):
        return (jnp.full((2 * t, 1), -jnp.inf, f32), jnp.zeros((2 * t, 1), f32),
                jnp.zeros((2 * t, LANES), f32))

    def finish(carry, t, rows):
        _, l, acc = carry
        o = acc[:t] / l[:t] - lam * (acc[t:] / l[t:])
        o_ref[rows, :] = _attn_finish(o, sub_ref[...], ag_ref[rows, :])

    meta = slice(0, N_META)
    qq = stack_maps(q_ref[meta, :])
    carry = step(qq, init(N_META), kb_ref[meta, :], vb_ref[meta, :], causal(N_META))
    finish(carry, N_META, meta)

    half = ATTN_TQ // 2
    row = lax.broadcasted_iota(jnp.int32, (4 * half, half), 0)
    col = lax.broadcasted_iota(jnp.int32, (4 * half, half), 1)
    mask_first_keys = (row >= 2 * half) | (jnp.bitwise_and(row, half - 1) >= col)
    mask_second_keys = causal(half)

    for i in range(n_tiles):
        r0 = N_META + i * ATTN_TQ
        top, bot = slice(r0, r0 + half), slice(r0 + half, r0 + ATTN_TQ)
        qq = jnp.concatenate([stack_maps(q_ref[top, :]), stack_maps(q_ref[bot, :])], axis=0)
        carry = step(qq, init(ATTN_TQ), kb_ref[meta, :], vb_ref[meta, :], None)
        for j in range(i):
            cols = slice(N_META + j * ATTN_TQ, N_META + (j + 1) * ATTN_TQ)
            carry = step(qq, carry, kb_ref[cols, :], vb_ref[cols, :], None)
        m, l, acc = step(qq, carry, kb_ref[top, :], vb_ref[top, :], mask_first_keys)
        finish((m[:2 * half], l[:2 * half], acc[:2 * half]), half, top)
        carry = step(qq[2 * half:], (m[2 * half:], l[2 * half:], acc[2 * half:]),
                     kb_ref[bot, :], vb_ref[bot, :], mask_second_keys)
        finish(carry, half, bot)


def _attn_prompt(z4, k, v, lams, subln, n_seq, seq_len):
    m = z4.shape[0]
    vec = pl.BlockSpec((1, DA_QKDIM), lambda b, h: (0, 0))
    return pl.pallas_call(
        _attn_prompt_kernel,
        grid=(n_seq, DA_HEADS),
        in_specs=[
            pl.BlockSpec((seq_len, LANES), lambda b, h: (b, 16 + h)),
            pl.BlockSpec((seq_len, LANES), lambda b, h: (b, h)),
            pl.BlockSpec((seq_len, LANES), lambda b, h: (b, h)),
            pl.BlockSpec((seq_len, LANES), lambda b, h: (b, 24 + h)),
            vec, vec, vec, vec,
            pl.BlockSpec((1, LANES), lambda b, h: (0, 0)),
        ],
        out_specs=pl.BlockSpec((seq_len, LANES), lambda b, h: (b, h)),
        out_shape=jax.ShapeDtypeStruct((m, 1024), bf16),
        scratch_shapes=[pltpu.VMEM((seq_len, LANES), bf16), pltpu.VMEM((seq_len, LANES), bf16)],
        compiler_params=_cparams(("arbitrary", "arbitrary")),
        name="attn_prompt",
    )(z4, k, v, z4, *lams, subln)


def _half_sum_matrix():
    d = np.arange(LANES)[:, None] < DA_QKDIM
    c = np.arange(LANES)[None, :] < DA_QKDIM
    return (d == c).astype(np.float32)


def _attn_sample_step(p, n_p, q_ref, kn_ref, vn_ref, ag_ref, lams, sub_ref, e_ref, kp_refs, vp_refs,
                      o_ref, m_ref, l_ref, acc_ref, accr_ref):
    pages = len(kp_refs)
    lq1, lk1, lq2, lk2 = lams
    q = q_ref[0] * LOG2E
    first_half = lax.broadcasted_iota(jnp.int32, (DA_HEADS, LANES), 1) < DA_QKDIM
    swap = lambda a: pltpu.roll(a, DA_QKDIM, a.ndim - 1)

    @pl.when(p == 0)
    def _():
        prod = kn_ref[0] * q
        s0 = jnp.sum(jnp.where(first_half, prod, 0.0), axis=-1, keepdims=True)
        s1 = jnp.sum(jnp.where(first_half, 0.0, prod), axis=-1, keepdims=True)
        m_ref[...] = jnp.where(first_half, s0, s1)
        l_ref[...] = jnp.ones_like(l_ref)
        acc_ref[...] = vn_ref[0]
        accr_ref[...] = swap(vn_ref[0])

    m, l, acc, acc_r = m_ref[...], l_ref[...], acc_ref[...], accr_ref[...]
    for pg in range(pages):
        for st in range(PAGE // ATTN_SUB):
            rows = slice(st * ATTN_SUB, (st + 1) * ATTN_SUB)
            kp = kp_refs[pg][0, rows]
            vp = vp_refs[pg][0, rows]
            prod = (kp * q[None]).reshape(ATTN_SUB * DA_HEADS, LANES).astype(bf16)
            r = jnp.dot(prod, e_ref[...], preferred_element_type=f32)
            r = r.reshape(ATTN_SUB, DA_HEADS, LANES)
            m_new = jnp.maximum(m, jnp.max(r, axis=0))
            alpha = jnp.exp2(m - m_new)
            pw = jnp.exp2(r - m_new[None])
            l = alpha * l + jnp.sum(pw, axis=0)
            acc = alpha * acc + jnp.sum(pw * vp, axis=0)
            acc_r = alpha * acc_r + jnp.sum(pw * swap(vp), axis=0)
            m = m_new
    m_ref[...], l_ref[...], acc_ref[...], accr_ref[...] = m, l, acc, acc_r

    @pl.when(p == n_p - 1)
    def _():
        lam = _lambda(lq1, lk1, lq2, lk2)
        other = swap(acc_r)
        l_sw = swap(l)
        o0 = jnp.where(first_half, acc, other) / jnp.where(first_half, l, l_sw)
        o1 = jnp.where(first_half, other, acc) / jnp.where(first_half, l_sw, l)
        o_ref[0] = _attn_finish(o0 - lam * o1, sub_ref[...], ag_ref[0])


def _attn_sample_kernel(pt_ref, q_ref, kn_ref, vn_ref, ag_ref, lq1, lk1, lq2, lk2, sub_ref, e_ref,
                        *rest, pages):
    del pt_ref
    _attn_sample_step(pl.program_id(1), pl.num_programs(1), q_ref, kn_ref, vn_ref, ag_ref,
                      (lq1, lk1, lq2, lk2), sub_ref, e_ref, rest[:pages], rest[pages:2 * pages],
                      *rest[2 * pages:])


def _attn_sample(page_table, q, k_new, v_new, ag, lams, subln, cache_k, cache_v, pages=ATTN_PAGES):
    bd, n_pages = page_table.shape
    assert n_pages % pages == 0
    tok = pl.BlockSpec((1, DA_HEADS, LANES), lambda b, p, pt: (b, 0, 0))
    vec = pl.BlockSpec((1, DA_QKDIM), lambda b, p, pt: (0, 0))

    def page(i):
        return pl.BlockSpec((1, PAGE, DA_HEADS, LANES),
                            lambda b, p, pt: (pt[b * n_pages + p * pages + i], 0, 0, 0))

    e = jnp.asarray(_half_sum_matrix(), dtype=bf16)
    grid_spec = pltpu.PrefetchScalarGridSpec(
        num_scalar_prefetch=1,
        grid=(bd, n_pages // pages),
        in_specs=[tok, tok, tok, tok, vec, vec, vec, vec,
                  pl.BlockSpec((1, LANES), lambda b, p, pt: (0, 0)),
                  pl.BlockSpec((LANES, LANES), lambda b, p, pt: (0, 0))]
                 + [page(i) for i in range(pages)] * 2,
        out_specs=tok,
        scratch_shapes=[pltpu.VMEM((DA_HEADS, LANES), f32) for _ in range(4)],
    )
    return pl.pallas_call(
        functools.partial(_attn_sample_kernel, pages=pages),
        grid_spec=grid_spec,
        out_shape=jax.ShapeDtypeStruct((bd, DA_HEADS, LANES), bf16),
        compiler_params=_cparams(("arbitrary", "arbitrary")),
        name="attn_sample",
    )(page_table.reshape(-1), q, k_new, v_new, ag, *lams, subln, e,
      *([cache_k] * pages), *([cache_v] * pages))


def _gla_consts(c):
    n = int(round(math.log2(c)))
    t = np.arange(c)[:, None]
    u = np.arange(c)[None, :]
    sums = [(u <= t), (u > t)]
    masks = [(u == t)]
    for lvl in range(n):
        half = c >> (lvl + 1)
        par = 2 * half
        split = (t // par) * par + half - 1
        upper = (t % par) >= half
        sums.append((upper & (u > split) & (u <= t)) | ((~upper) & (u > t) & (u <= split)))
        masks.append(upper & ((u % par) < half) & ((u // par) == (t // par)))
    sums = np.stack(sums).astype(np.float32).reshape(-1, c)
    return np.concatenate([sums] * 3, axis=1), np.stack(masks).astype(np.float32)


def _split3(x):
    hi = x.astype(bf16)
    r = x - hi.astype(f32)
    mid = r.astype(bf16)
    lo = (r - mid.astype(f32)).astype(bf16)
    return jnp.concatenate([hi, mid, lo], axis=0)


def _log_sigmoid(x):
    return jnp.minimum(x, 0.0) - jnp.log1p(jnp.exp(-jnp.abs(x)))


def _log_decay(lr, wa, ba):
    x = jnp.dot(lr.astype(bf16), wa, preferred_element_type=f32) + ba
    return _log_sigmoid(x) * (1.0 / GLA_TAU)


def _gla_finish(o, gn, gate):
    return (_rms_rows(o, gn) * (gate * _sigmoid(gate))).astype(bf16)


def _gla_prompt_kernel(q_ref, k_ref, v_ref, gate_ref, lr_ref, wa_ref, ba_ref, gn_ref,
                       sum16_ref, msk16_ref, sum64_ref, msk64_ref, o_ref, st_ref, g_ref, s_ref):
    g_ref[...] = _log_decay(lr_ref[...], wa_ref[...], ba_ref[...]) * LOG2E

    def chunk(st, r0, c, sum_ref, msk_ref):
        n = msk_ref.shape[0] - 1
        rows = pl.ds(r0, c)
        q = q_ref[rows, :]
        k = k_ref[rows, :]
        vb = v_ref[rows, :].astype(bf16)
        e = jnp.exp2(jnp.dot(sum_ref[...], _split3(g_ref[rows, :]),
                             preferred_element_type=f32))
        o = lax.dot_general((q * e[0:c]).astype(bf16), st.astype(bf16), NT_DIMS,
                            preferred_element_type=f32)
        sc = msk_ref[0] * lax.dot_general(q.astype(bf16), k.astype(bf16), NT_DIMS,
                                          preferred_element_type=f32)
        for lvl in range(n):
            el = e[(2 + lvl) * c:(3 + lvl) * c]
            sc = sc + msk_ref[1 + lvl] * lax.dot_general(
                (q * el).astype(bf16), (k * el).astype(bf16), NT_DIMS, preferred_element_type=f32)
        o = o + jnp.dot(sc.astype(bf16), vb, preferred_element_type=f32)
        k_end = (k * e[c:2 * c]).astype(bf16)
        o_ref[rows, :] = _gla_finish(o, gn_ref[...], gate_ref[rows, :])
        return st * e[c - 1:c] + lax.dot_general(vb, k_end, TN_DIMS, preferred_element_type=f32)

    s_ref[...] = chunk(jnp.zeros(s_ref.shape, f32), 0, N_META, sum16_ref, msk16_ref)

    def body(ci, _):
        st = s_ref[...]
        for u in range(GLA_UNROLL):
            r0 = pl.multiple_of(N_META + (ci * GLA_UNROLL + u) * GLA_CHUNK, 16)
            st = chunk(st, r0, GLA_CHUNK, sum64_ref, msk64_ref)
        s_ref[...] = st
        return 0

    n_chunks = (q_ref.shape[0] - N_META) // GLA_CHUNK
    assert n_chunks % GLA_UNROLL == 0
    lax.fori_loop(0, n_chunks // GLA_UNROLL, body, 0)
    st_ref[0, 0] = s_ref[...].T


def _gla_prompt(z1, lr, wa, ba, gn, n_seq, seq_len):
    m = z1.shape[0]
    sum16, msk16 = _gla_consts(N_META)
    sum64, msk64 = _gla_consts(GLA_CHUNK)
    consts = [jnp.asarray(sum16, bf16), jnp.asarray(msk16), jnp.asarray(sum64, bf16),
              jnp.asarray(msk64)]

    def full(a):
        nd = a.ndim
        return pl.BlockSpec(a.shape, lambda b, h: (0,) * nd)

    return pl.pallas_call(
        _gla_prompt_kernel,
        grid=(n_seq, GLA_HEADS),
        in_specs=[
            pl.BlockSpec((seq_len, GLA_KDIM), lambda b, h: (b, h)),
            pl.BlockSpec((seq_len, GLA_KDIM), lambda b, h: (b, 4 + h)),
            pl.BlockSpec((seq_len, GLA_VDIM), lambda b, h: (b, 4 + h)),
            pl.BlockSpec((seq_len, GLA_VDIM), lambda b, h: (b, 8 + h)),
            pl.BlockSpec((seq_len, LANES), lambda b, h: (b, 0)),
            pl.BlockSpec((LANES, GLA_KDIM), lambda b, h: (0, h)),
            pl.BlockSpec((1, GLA_KDIM), lambda b, h: (0, h)),
            pl.BlockSpec((1, GLA_VDIM), lambda b, h: (0, 0)),
        ] + [full(a) for a in consts],
        out_specs=[
            pl.BlockSpec((seq_len, GLA_VDIM), lambda b, h: (b, h)),
            pl.BlockSpec((1, 1, GLA_KDIM, GLA_VDIM), lambda b, h: (b, h, 0, 0)),
        ],
        out_shape=[
            jax.ShapeDtypeStruct((m, GLA_HEADS * GLA_VDIM), bf16),
            jax.ShapeDtypeStruct((n_seq, GLA_HEADS, GLA_KDIM, GLA_VDIM), f32),
        ],
        scratch_shapes=[pltpu.VMEM((seq_len, GLA_KDIM), f32), pltpu.VMEM((GLA_VDIM, GLA_KDIM), f32)],
        compiler_params=_cparams(("arbitrary", "arbitrary")),
        name="gla_prompt",
    )(z1, z1, z1, z1, lr, wa, ba, gn, *consts)


def _gla_sample_kernel(z_ref, lr_ref, wa_ref, wat_ref, ba_ref, bat_ref, gn_ref, s_ref,
                       o_ref, st_ref):
    first_row = lax.broadcasted_iota(jnp.int32, (16, 1), 0) == 0
    kw, vw = GLA_HEADS * GLA_KDIM, GLA_HEADS * GLA_VDIM
    for b in range(z_ref.shape[0]):
        lr = lr_ref[b].astype(bf16)
        lr_rows = jnp.broadcast_to(lr, (LANES, LANES))
        for h in range(GLA_HEADS):
            kc = slice(h * GLA_KDIM, (h + 1) * GLA_KDIM)
            vc = slice(h * GLA_VDIM, (h + 1) * GLA_VDIM)
            q = z_ref[b, :, kc]
            k = z_ref[b, :, kw + h * GLA_KDIM:kw + (h + 1) * GLA_KDIM]
            v = z_ref[b, :, 2 * kw + h * GLA_VDIM:2 * kw + (h + 1) * GLA_VDIM]
            gate = z_ref[b, :, 2 * kw + vw + h * GLA_VDIM:2 * kw + vw + (h + 1) * GLA_VDIM]
            x_row = jnp.dot(lr, wa_ref[:, kc], preferred_element_type=f32) + ba_ref[:, kc]
            x_col = lax.dot_general(wat_ref[kc, :], lr_rows, NT_DIMS,
                                    preferred_element_type=f32) + bat_ref[kc, :]
            eg_row = jnp.exp(_log_sigmoid(x_row) * (1.0 / GLA_TAU))
            eg_col = jnp.exp(_log_sigmoid(x_col) * (1.0 / GLA_TAU))
            s = s_ref[b, h]
            o = jnp.dot((q * eg_row).astype(bf16), s.astype(bf16), preferred_element_type=f32)
            o = o + jnp.sum(q * k, axis=-1, keepdims=True) * v
            k16 = jnp.where(first_row, k, 0.0).astype(bf16)
            v16 = jnp.broadcast_to(v, (16, GLA_VDIM)).astype(bf16)
            st_ref[b, h] = (s * jnp.concatenate([eg_col] * (GLA_VDIM // LANES), axis=1)
                            + lax.dot_general(k16, v16, TN_DIMS, preferred_element_type=f32))
            o_ref[b, :, vc] = _gla_finish(o, gn_ref[...], gate)


def _gla_sample(z1, lr, wa, ba, gn, state):
    bd = state.shape[0]
    assert bd % GLA_SEQS == 0
    kw = GLA_HEADS * GLA_KDIM
    wat = wa.T
    bat = jnp.broadcast_to(ba.reshape(kw, 1), (kw, LANES))
    const = lambda a: pl.BlockSpec(a.shape, lambda b: (0,) * a.ndim)
    st_spec = pl.BlockSpec((GLA_SEQS, GLA_HEADS, GLA_KDIM, GLA_VDIM), lambda b: (b, 0, 0, 0))
    return pl.pallas_call(
        _gla_sample_kernel,
        grid=(bd // GLA_SEQS,),
        in_specs=[
            pl.BlockSpec((GLA_SEQS, 1, z1.shape[2]), lambda b: (b, 0, 0)),
            pl.BlockSpec((GLA_SEQS, 1, LANES), lambda b: (b, 0, 0)),
            const(wa), const(wat), const(ba), const(bat), const(gn), st_spec,
        ],
        out_specs=[
            pl.BlockSpec((GLA_SEQS, 1, GLA_HEADS * GLA_VDIM), lambda b: (b, 0, 0)),
            st_spec,
        ],
        out_shape=[
            jax.ShapeDtypeStruct((bd, 1, GLA_HEADS * GLA_VDIM), bf16),
            jax.ShapeDtypeStruct(state.shape, f32),
        ],
        compiler_params=_cparams(("arbitrary",)),
        name="gla_sample",
    )(z1, lr, wa, wat, ba, bat, gn, state)


def _rope_tables(positions):
    half = ROT_DIM // 2
    inv = jnp.power(ROPE_THETA, -jnp.arange(half, dtype=f32) * 2.0 / ROT_DIM)
    ang = positions.astype(f32)[:, None] * inv[None, :]
    cos, sin = jnp.cos(ang), jnp.sin(ang)
    t = positions.shape[0]
    rest = DA_QKDIM - ROT_DIM
    z8 = jnp.zeros((t, half), f32)
    zr = jnp.zeros((t, rest), f32)
    c = jnp.concatenate([cos, cos, jnp.ones((t, rest), f32)], axis=1)
    s1 = jnp.concatenate([z8, sin, zr], axis=1)
    s2 = jnp.concatenate([-sin, z8, zr], axis=1)
    return tuple(jnp.concatenate([a, a], axis=1) for a in (c, s1, s2))


def kernel(x_prompt, x_sample, cache_k, cache_v, state_pool, state_gla, page_table, meta_tokens,
           pre_norm_0, post_norm_0, w_in_0, pool_w_0, pool_scale_0, lambda_q1_0, lambda_k1_0,
           lambda_q2_0, lambda_k2_0, subln_0, w_out_0,
           pre_norm_1, post_norm_1, w_in_1, gla_wa2_1, gla_ba_1, gla_norm_1, w_out_1):
    B, seq = x_prompt.shape[0], x_prompt.shape[1]
    Bd = x_sample.shape[0]
    L = seq + N_META
    assert L % ROW_TILE == 0 and (L - N_META) % ATTN_TQ == 0 and x_sample.shape[1] == 1

    row = lambda a: a.reshape(1, -1)
    w0 = w_in_0.astype(bf16)
    w1 = w_in_1.astype(bf16)
    w1_lr = jnp.pad(w_in_1[:, 6144:], ((0, 0), (0, LANES - GLA_RANK))).astype(bf16)
    wa = jnp.pad(gla_wa2_1, ((0, LANES - GLA_RANK), (0, 0))).astype(bf16)
    wo0 = w_out_0.astype(bf16)
    wo1 = w_out_1.astype(bf16)
    pw = pool_w_0.astype(bf16)
    lams = [row(a) for a in (lambda_q1_0, lambda_k1_0, lambda_q2_0, lambda_k2_0)]

    meta = meta_tokens.astype(x_prompt.dtype)
    x_p = x_prompt.reshape(B * seq, D_MODEL)
    h_s = x_sample.reshape(Bd, D_MODEL)
    tiles = L // ROW_TILE
    tabs_p = _rope_tables(jnp.arange(L, dtype=jnp.int32))
    past_len = page_table.shape[1] * PAGE
    tabs_s = _rope_tables(jnp.full((Bd,), past_len, dtype=jnp.int32))

    z4_p, k_p, v_p = _inproj0(x_p, row(pre_norm_0), w0, tabs_p, ROW_TILE, tiles, meta=meta, n_seq=B)
    mix_b = _attn_prompt(z4_p, k_p, v_p, lams, row(subln_0), B, L)
    h_p = _outproj0_prompt(z4_p, pw, row(pool_scale_0), mix_b, wo0, x_p, meta, row(post_norm_0),
                           ROW_TILE, tiles, seq)

    z4_s, k_s, v_s = _inproj0(h_s, row(pre_norm_0), w0, tabs_s, Bd, 1)
    mix_a_s = _pool_sample(z4_s, state_pool, pw, row(pool_scale_0))
    hd = lambda a: a.reshape(Bd, DA_HEADS, LANES)

    mix_b_s = _attn_sample(page_table, hd(z4_s[:, 2048:3072]), hd(k_s), hd(v_s),
                           hd(z4_s[:, 3072:4096]), lams, row(subln_0), cache_k, cache_v)
    h_s = _outproj([mix_a_s, mix_b_s.reshape(Bd, 1024)], wo0, h_s, row(post_norm_0), Bd)

    z1_p, lr_p = _inproj1(h_p, row(pre_norm_1), w1, w1_lr, ROW_TILE)
    mix1_p, gla_p = _gla_prompt(z1_p, lr_p, wa, row(gla_ba_1), row(gla_norm_1), B, L)
    per_seq = seq // OUT_TILE
    y_p = _outproj([mix1_p], wo1, h_p, row(post_norm_1), OUT_TILE, n_tiles=B * per_seq,
                   row_offset=lambda i: pl.multiple_of(
                       (i // per_seq) * L + N_META + (i % per_seq) * OUT_TILE, 16))

    z1_s, lr_s = _inproj1(h_s, row(pre_norm_1), w1, w1_lr, Bd)
    mix1_s, gla_s = _gla_sample(z1_s.reshape(Bd, 1, 6144), lr_s.reshape(Bd, 1, LANES), wa,
                                row(gla_ba_1), row(gla_norm_1), state_gla)
    h_s = _outproj([mix1_s.reshape(Bd, 2048)], wo1, h_s, row(post_norm_1), Bd)

    pool_p = lax.slice(z4_p.reshape(B, L, 4096), (0, L - POOL_CTX, 0), (B, L, 1024))
    pool_s = jnp.concatenate([state_pool[:, 1:], z4_s[:, None, :1024]], axis=1)
    return (y_p.reshape(B, seq, D_MODEL), h_s.reshape(Bd, 1, D_MODEL),
            k_p.reshape(B, L, DA_HEADS, LANES), v_p.reshape(B, L, DA_HEADS, LANES), pool_p, gla_p,
            k_s.reshape(Bd, 1, DA_HEADS, LANES), v_s.reshape(Bd, 1, DA_HEADS, LANES), pool_s, gla_s)
```
